```python
import math
import jax, jax.numpy as jnp
from jax import lax
import numpy as np

D_MODEL = 1024
BATCH = 8
SEQ = 8192
DEPTH = 1

MIX_WIDTH = D_MODEL
SGU_GROUPS = 4
SGU_CH = D_MODEL // 8
SGU_WIDTH = SGU_GROUPS * SGU_CH
CHUNK = 128
DIFF_HEADS = 4
DIFF_QK_DIM = D_MODEL // 16
DIFF_V_DIM = 2 * DIFF_QK_DIM
DIFF_QK_WIDTH = DIFF_HEADS * 2 * DIFF_QK_DIM
DIFF_WIDTH = DIFF_HEADS * DIFF_V_DIM
Q_BLOCK = 128
IN_WIDTH = 2 * SGU_WIDTH + 2 * DIFF_QK_WIDTH + DIFF_WIDTH
MEM_LEN = 256
MEM_HEADS = 4
MEM_HEAD_DIM = D_MODEL // MEM_HEADS
MEM_WIDTH = MEM_HEADS * MEM_HEAD_DIM
N_GROUPS = 4
EXPERTS_PER_GROUP = 8
N_EXPERTS = N_GROUPS * EXPERTS_PER_GROUP
TOP_K = 2
D_EXPERT = D_MODEL // 2
EXPERT_BLOCK = 128
RMS_EPS = 1e-6
LN_EPS = 1e-5

kernel_name = "hymba_sgu_diffattn_alibi_memxattn_hmoe"


def rms_norm(x, g, eps=RMS_EPS):
    xf = x.astype(jnp.float32)
    y = xf * lax.rsqrt(jnp.mean(xf * xf, axis=-1, keepdims=True) + eps)
    return (y * g.astype(jnp.float32)).astype(x.dtype)


def layer_norm(x, g, b, eps=LN_EPS):
    xf = x.astype(jnp.float32)
    mu = jnp.mean(xf, axis=-1, keepdims=True)
    var = jnp.mean(jnp.square(xf - mu), axis=-1, keepdims=True)
    y = (xf - mu) * lax.rsqrt(var + eps)
    return (y * g.astype(jnp.float32) + b.astype(jnp.float32)).astype(x.dtype)


def chunked_sgu(u, v, ln_g, ln_b, w_s, b_s):
    Bn, Sn, G, C = v.shape
    vn = layer_norm(v, ln_g, ln_b)
    vc = vn.reshape(Bn, Sn // CHUNK, CHUNK, G, C)
    ws = jnp.tril(w_s).astype(vc.dtype)
    sp = jnp.einsum('gts,bnsgc->bntgc', ws, vc) + b_s.T.astype(vc.dtype)[:, :, None]
    return u * sp.reshape(Bn, Sn, G, C)


def diff_attention(q, k, v, lam):
    Bn, H, _, Sn, Dq = q.shape
    n_qb = Sn // Q_BLOCK
    slopes = 2.0 ** (-8.0 * jnp.arange(1, H + 1, dtype=jnp.float32) / H)
    q_blocks = jnp.moveaxis(q.reshape(Bn, H, 2, n_qb, Q_BLOCK, Dq), 3, 0)
    kpos = jnp.arange(Sn, dtype=jnp.int32)

    def attend(args):
        qb, i = args
        s = jnp.einsum('bhiqd,bhikd->bhiqk', qb, k).astype(jnp.float32)
        qpos = i * Q_BLOCK + jnp.arange(Q_BLOCK, dtype=jnp.int32)
        dist = (qpos[:, None] - kpos[None, :]).astype(jnp.float32)
        s = s - (slopes[:, None, None] * dist)[None, :, None]
        s = jnp.where(dist >= 0, s, -jnp.inf)
        p = jax.nn.softmax(s, axis=-1)
        a = p[:, :, 0] - lam * p[:, :, 1]
        return jnp.einsum('bhqk,bhkd->bqhd', a.astype(v.dtype), v)

    o = lax.map(attend, (q_blocks, jnp.arange(n_qb, dtype=jnp.int32)))
    return jnp.moveaxis(o, 0, 1).reshape(Bn, Sn, H, v.shape[-1])


def memory_attention(hq, mem_n, w_q, w_kv, w_o):
    Bn, Sn, _ = hq.shape
    Mn = mem_n.shape[1]
    q = (hq @ w_q).reshape(Bn, Sn, MEM_HEADS, MEM_HEAD_DIM) * (MEM_HEAD_DIM ** -0.5)
    kv = (mem_n @ w_kv).reshape(Bn, Mn, 2, MEM_HEADS, MEM_HEAD_DIM)
    k, v = kv[:, :, 0], kv[:, :, 1]
    s = jnp.einsum('bqhd,bkhd->bhqk', q, k).astype(jnp.float32)
    p = jax.nn.softmax(s, axis=-1)
    o = jnp.einsum('bhqk,bkhd->bqhd', p.astype(v.dtype), v).reshape(Bn, Sn, MEM_WIDTH)
    return o @ w_o


def hierarchical_moe(h, w_rg, b_rg, w_re, b_re, w_gate, w_up, w_down):
    Bn, Sn, D = h.shape
    t = h.reshape(-1, D)
    N = t.shape[0]
    pg = jax.nn.softmax((t @ w_rg + b_rg).astype(jnp.float32), axis=-1)
    gate_g, gidx = lax.top_k(pg, 1)
    el = (t @ w_re + b_re).astype(jnp.float32).reshape(N, N_GROUPS, EXPERTS_PER_GROUP)
    el = el[jnp.arange(N), gidx[:, 0]]
    pe = jax.nn.softmax(el, axis=-1)
    pe_top, eloc = lax.top_k(pe, TOP_K)
    wts = gate_g * pe_top / jnp.sum(pe_top, axis=-1, keepdims=True)
    eid = (gidx * EXPERTS_PER_GROUP + eloc).reshape(-1)
    A = N * TOP_K
    tok = jnp.repeat(jnp.arange(N, dtype=jnp.int32), TOP_K)
    order = jnp.argsort(eid)
    e_sorted = eid[order]
    counts = jnp.bincount(eid, length=N_EXPERTS)
    starts = jnp.cumsum(counts) - counts
    padded = (counts + EXPERT_BLOCK - 1) // EXPERT_BLOCK * EXPERT_BLOCK
    pend = jnp.cumsum(padded)
    pstarts = pend - padded
    pos_sorted = (pstarts[e_sorted] + jnp.arange(A, dtype=jnp.int32) - starts[e_sorted]).astype(jnp.int32)
    P = -(-A // EXPERT_BLOCK) * EXPERT_BLOCK + N_EXPERTS * EXPERT_BLOCK
    n_blocks = P // EXPERT_BLOCK
    row_tok = jnp.zeros((P,), jnp.int32).at[pos_sorted].set(tok[order])
    block_expert = jnp.clip(jnp.searchsorted(pend, jnp.arange(n_blocks, dtype=jnp.int32) * EXPERT_BLOCK,
                                             side='right'), 0, N_EXPERTS - 1).astype(jnp.int32)
    x_rows = t[row_tok].reshape(n_blocks, EXPERT_BLOCK, D)

    def expert_rows(args):
        xb, e = args
        g = xb @ w_gate[e]
        u = xb @ w_up[e]
        return (jax.nn.silu(g) * u) @ w_down[e]

    y_rows = lax.map(expert_rows, (x_rows, block_expert)).reshape(P, D)
    pos = jnp.zeros((A,), jnp.int32).at[order].set(pos_sorted)
    y = y_rows[pos].reshape(N, TOP_K, D)
    out = jnp.einsum('nk,nkd->nd', wts.astype(y.dtype), y)
    return out.reshape(Bn, Sn, D)


def setup_inputs(seed: int = 0) -> dict:
    key = jax.random.key(seed)
    ks = jax.random.split(key, 32)
    L, D = DEPTH, D_MODEL
    f32 = jnp.float32

    def nrm(k, shape, scale):
        return jax.random.normal(k, shape, f32) * scale

    def gain(k, shape):
        return 1.0 + 0.05 * jax.random.normal(k, shape, f32)

    return {
        "x": jax.random.normal(ks[0], (BATCH, SEQ, D), f32),
        "mem": jax.random.normal(ks[1], (BATCH, MEM_LEN, D), f32),
        "norm_mix": gain(ks[2], (L, D)),
        "w_in": nrm(ks[3], (L, D, IN_WIDTH), D ** -0.5),
        "sgu_ln_g": gain(ks[4], (L, SGU_GROUPS, SGU_CH)),
        "sgu_ln_b": nrm(ks[5], (L, SGU_GROUPS, SGU_CH), 0.02),
        "sgu_w": nrm(ks[6], (L, SGU_GROUPS, CHUNK, CHUNK), CHUNK ** -0.5),
        "sgu_b": gain(ks[7], (L, SGU_GROUPS, CHUNK)),
        "lambda_q1": nrm(ks[8], (L, DIFF_QK_DIM), 0.1),
        "lambda_k1": nrm(ks[9], (L, DIFF_QK_DIM), 0.1),
        "lambda_q2": nrm(ks[10], (L, DIFF_QK_DIM), 0.1),
        "lambda_k2": nrm(ks[11], (L, DIFF_QK_DIM), 0.1),
        "diff_subln": gain(ks[12], (L, DIFF_V_DIM)),
        "w_out": nrm(ks[13], (L, MIX_WIDTH, D), MIX_WIDTH ** -0.5),
        "norm_xq": gain(ks[14], (L, D)),
        "norm_mem": gain(ks[15], (L, D)),
        "w_q_mem": nrm(ks[16], (L, D, MEM_WIDTH), D ** -0.5),
        "w_kv_mem": nrm(ks[17], (L, D, 2 * MEM_WIDTH), D ** -0.5),
        "w_o_mem": nrm(ks[18], (L, MEM_WIDTH, D), MEM_WIDTH ** -0.5),
        "norm_ffn": gain(ks[19], (L, D)),
        "w_router_group": nrm(ks[20], (L, D, N_GROUPS), D ** -0.5),
        "b_router_group": nrm(ks[21], (L, N_GROUPS), 0.01),
        "w_router_expert": nrm(ks[22], (L, D, N_EXPERTS), D ** -0.5),
        "b_router_expert": nrm(ks[23], (L, N_EXPERTS), 0.01),
        "w_gate": nrm(ks[24], (L, N_EXPERTS, D, D_EXPERT), D ** -0.5),
        "w_up": nrm(ks[25], (L, N_EXPERTS, D, D_EXPERT), D ** -0.5),
        "w_down": nrm(ks[26], (L, N_EXPERTS, D_EXPERT, D), D_EXPERT ** -0.5),
        "norm_final": gain(ks[27], (D,)),
    }


def reference(x, mem, norm_mix, w_in, sgu_ln_g, sgu_ln_b, sgu_w, sgu_b, lambda_q1, lambda_k1, lambda_q2,
              lambda_k2, diff_subln, w_out, norm_xq, norm_mem, w_q_mem, w_kv_mem, w_o_mem, norm_ffn,
              w_router_group, b_router_group, w_router_expert, b_router_expert, w_gate, w_up, w_down,
              norm_final):
    Bn, Sn, _ = x.shape
    cuts = [SGU_WIDTH, 2 * SGU_WIDTH, 2 * SGU_WIDTH + DIFF_QK_WIDTH, 2 * SGU_WIDTH + 2 * DIFF_QK_WIDTH]
    for l in range(DEPTH):
        lambda_init = 0.8 - 0.6 * math.exp(-0.3 * l)
        h = rms_norm(x, norm_mix[l])
        z = h @ w_in[l]
        zu, zv, zq, zk, zval = jnp.split(z, cuts, axis=-1)
        u = jax.nn.gelu(zu, approximate=False).reshape(Bn, Sn, SGU_GROUPS, SGU_CH)
        v = jax.nn.gelu(zv, approximate=False).reshape(Bn, Sn, SGU_GROUPS, SGU_CH)
        a_out = chunked_sgu(u, v, sgu_ln_g[l], sgu_ln_b[l], sgu_w[l], sgu_b[l]).reshape(Bn, Sn, SGU_WIDTH)
        q = zq.reshape(Bn, Sn, DIFF_HEADS, 2, DIFF_QK_DIM).transpose(0, 2, 3, 1, 4) * (DIFF_QK_DIM ** -0.5)
        k = zk.reshape(Bn, Sn, DIFF_HEADS, 2, DIFF_QK_DIM).transpose(0, 2, 3, 1, 4)
        vv = zval.reshape(Bn, Sn, DIFF_HEADS, DIFF_V_DIM).transpose(0, 2, 1, 3)
        lam = (jnp.exp(jnp.sum(lambda_q1[l].astype(jnp.float32) * lambda_k1[l].astype(jnp.float32)))
               - jnp.exp(jnp.sum(lambda_q2[l].astype(jnp.float32) * lambda_k2[l].astype(jnp.float32)))
               + lambda_init)
        o = diff_attention(q, k, vv, lam)
        o = rms_norm(o, diff_subln[l]) * (1.0 - lambda_init)
        mixed = jnp.concatenate([a_out, o.reshape(Bn, Sn, DIFF_WIDTH)], axis=-1) @ w_out[l]
        x = x + mixed
        x = x + memory_attention(rms_norm(x, norm_xq[l]), rms_norm(mem, norm_mem[l]),
                                 w_q_mem[l], w_kv_mem[l], w_o_mem[l])
        x = x + hierarchical_moe(rms_norm(x, norm_ffn[l]), w_router_group[l], b_router_group[l],
                                 w_router_expert[l], b_router_expert[l], w_gate[l], w_up[l], w_down[l])
    return rms_norm(x, norm_final)
```

```python
import functools
import math

import jax
import jax.numpy as jnp
from jax import lax
from jax.experimental import pallas as pl
from jax.experimental.pallas import tpu as pltpu

F32 = jnp.float32
BF16 = jnp.bfloat16

LANES = 128
SGU_GROUPS = 4
SGU_CH = 128
CHUNK = 128
DIFF_HEADS = 4
DIFF_QK_DIM = 64
DIFF_V_DIM = 128
MEM_HEADS = 4
N_GROUPS = 4
EXPERTS_PER_GROUP = 8
N_EXPERTS = N_GROUPS * EXPERTS_PER_GROUP
TOP_K = 2
RMS_EPS = 1e-6
LN_EPS = 1e-5
LAMBDA_INIT = 0.8 - 0.6 * math.exp(-0.3 * 0)

TOKEN_TILE = 512
ATTN_TILE = 256
EXPERT_ROWS = 512
VMEM_LIMIT = 56 * 1024 * 1024


def _cparams(*sem):
    return pltpu.CompilerParams(dimension_semantics=sem, vmem_limit_bytes=VMEM_LIMIT)


def _gelu(x):
    return 0.5 * x * (1.0 + lax.erf(x * (2.0 ** -0.5)))


def _rms(x, g):
    ms = jnp.mean(x * x, axis=-1, keepdims=True)
    return x * lax.rsqrt(ms + RMS_EPS) * g


def _inproj_sgu_kernel(x_ref, g_ref, w_ref, lng_ref, lnb_ref, ws_ref, bs_ref,
                       a_ref, q_ref, k_ref, v_ref):
    tm = x_ref.shape[0]
    h = _rms(x_ref[...], g_ref[...]).astype(BF16)

    def proj(c0, c1):
        return jnp.dot(h, w_ref[:, c0:c1], preferred_element_type=F32)

    sw = SGU_GROUPS * SGU_CH
    qkw = DIFF_HEADS * 2 * DIFF_QK_DIM
    u = _gelu(proj(0, sw))
    v = _gelu(proj(sw, 2 * sw))
    q_ref[...] = (proj(2 * sw, 2 * sw + qkw) * (DIFF_QK_DIM ** -0.5)).astype(BF16)
    k_ref[...] = proj(2 * sw + qkw, 2 * sw + 2 * qkw).astype(BF16)
    v_ref[...] = proj(2 * sw + 2 * qkw, 2 * sw + 2 * qkw + DIFF_HEADS * DIFF_V_DIM).astype(BF16)

    row = lax.broadcasted_iota(jnp.int32, (CHUNK, CHUNK), 0)
    col = lax.broadcasted_iota(jnp.int32, (CHUNK, CHUNK), 1)
    nchunk = tm // CHUNK
    for g in range(SGU_GROUPS):
        cs = slice(g * SGU_CH, (g + 1) * SGU_CH)
        vg = v[:, cs]
        mu = jnp.mean(vg, axis=-1, keepdims=True)
        d = vg - mu
        var = jnp.mean(d * d, axis=-1, keepdims=True)
        vn = (d * lax.rsqrt(var + LN_EPS) * lng_ref[g:g + 1, :] + lnb_ref[g:g + 1, :]).astype(BF16)
        rhs = jnp.concatenate([vn[c * CHUNK:(c + 1) * CHUNK, :] for c in range(nchunk)], axis=1)
        wt = jnp.where(row >= col, ws_ref[g], 0.0).astype(BF16)
        sp = jnp.dot(wt, rhs, preferred_element_type=F32) + bs_ref[:, g:g + 1]
        for c in range(nchunk):
            rs = slice(c * CHUNK, (c + 1) * CHUNK)
            a_ref[rs, cs] = (u[rs, cs] * sp[:, c * SGU_CH:(c + 1) * SGU_CH]).astype(BF16)


def _inproj_sgu(x2d, norm_mix, w_in, ln_g, ln_b, w_s, b_s_t):
    n, d = x2d.shape
    tm = TOKEN_TILE
    wide = DIFF_HEADS * DIFF_V_DIM
    row_spec = pl.BlockSpec((tm, wide), lambda i: (i, 0))
    full = lambda a: pl.BlockSpec(a.shape, lambda i: (0,) * a.ndim)
    out_sds = jax.ShapeDtypeStruct((n, wide), BF16)
    return pl.pallas_call(
        _inproj_sgu_kernel,
        grid=(n // tm,),
        in_specs=[pl.BlockSpec((tm, d), lambda i: (i, 0)), full(norm_mix), full(w_in),
                  full(ln_g), full(ln_b), full(w_s), full(b_s_t)],
        out_specs=[row_spec] * 4,
        out_shape=[out_sds] * 4,
        compiler_params=_cparams("parallel"),
        name="inproj_sgu",
    )(x2d, norm_mix, w_in, ln_g, ln_b, w_s, b_s_t)


def _diff_attn_kernel(q_ref, k_ref, v_ref, lamv_ref, sub_ref, o_ref, m_scr, l_scr, acc_scr):
    t = q_ref.shape[1]
    h = pl.program_id(1)
    iq = pl.program_id(2)
    slope = jnp.exp2(-(8.0 / DIFF_HEADS) * jnp.full((1, 1), h + 1, jnp.int32).astype(F32))

    q = q_ref[0]
    lane = lax.broadcasted_iota(jnp.int32, (t, 2 * DIFF_QK_DIM), 1)
    zero = jnp.zeros_like(q)
    q2 = jnp.concatenate([jnp.where(lane < DIFF_QK_DIM, q, zero),
                          jnp.where(lane >= DIFF_QK_DIM, q, zero)], axis=0)
    rel = (lax.broadcasted_iota(jnp.int32, (2 * t, t), 0) % t
           - lax.broadcasted_iota(jnp.int32, (2 * t, t), 1)).astype(F32)
    nbias = -slope * rel

    m_scr[...] = jnp.full(m_scr.shape, -jnp.inf, F32)
    l_scr[...] = jnp.zeros(l_scr.shape, F32)
    acc_scr[...] = jnp.zeros(acc_scr.shape, F32)

    def block(j, masked):
        off = pl.multiple_of(j * t, t)
        kb = k_ref[0, pl.ds(off, t), :]
        vb = v_ref[0, pl.ds(off, t), :]
        s = lax.dot_general(q2, kb, (((1,), (1,)), ((), ())), preferred_element_type=F32)
        s = s + nbias
        if masked:
            s = jnp.where(rel >= 0, s, -jnp.inf)
        cj = -slope * ((iq - j) * t).astype(F32)
        m_prev = m_scr[:, :1]
        m_new = jnp.maximum(m_prev, jnp.max(s, axis=-1, keepdims=True) + cj)
        p = jnp.exp(s - (m_new - cj))
        alpha = jnp.exp(m_prev - m_new)
        l_scr[...] = alpha * l_scr[...] + jnp.sum(p, axis=-1, keepdims=True)
        acc_scr[...] = alpha * acc_scr[...] + jnp.dot(p.astype(BF16), vb, preferred_element_type=F32)
        m_scr[...] = jnp.broadcast_to(m_new, m_scr.shape)

    def body(j, c):
        block(j, False)
        return c

    lax.fori_loop(0, iq, body, 0)
    block(iq, True)

    lam = (jnp.exp(jnp.sum(lamv_ref[0:1, :] * lamv_ref[1:2, :], axis=-1, keepdims=True))
           - jnp.exp(jnp.sum(lamv_ref[2:3, :] * lamv_ref[3:4, :], axis=-1, keepdims=True))
           + LAMBDA_INIT)
    o = acc_scr[...] / l_scr[...]
    o = o[:t] - lam * o[t:]
    o_ref[0] = (_rms(o, sub_ref[...]) * (1.0 - LAMBDA_INIT)).astype(o_ref.dtype)


def _diff_attn(q, k, v, lamv, subln):
    b, s, _ = q.shape
    t = ATTN_TILE
    dv = DIFF_V_DIM
    return pl.pallas_call(
        _diff_attn_kernel,
        grid=(b, DIFF_HEADS, s // t),
        in_specs=[pl.BlockSpec((1, t, 2 * DIFF_QK_DIM), lambda bi, hi, qi: (bi, qi, hi)),
                  pl.BlockSpec((1, s, 2 * DIFF_QK_DIM), lambda bi, hi, qi: (bi, 0, hi)),
                  pl.BlockSpec((1, s, dv), lambda bi, hi, qi: (bi, 0, hi)),
                  pl.BlockSpec(lamv.shape, lambda bi, hi, qi: (0, 0)),
                  pl.BlockSpec(subln.shape, lambda bi, hi, qi: (0, 0))],
        out_specs=pl.BlockSpec((1, t, dv), lambda bi, hi, qi: (bi, qi, hi)),
        out_shape=jax.ShapeDtypeStruct((b, s, DIFF_HEADS * dv), BF16),
        scratch_shapes=[pltpu.VMEM((2 * t, LANES), F32), pltpu.VMEM((2 * t, LANES), F32),
                        pltpu.VMEM((2 * t, dv), F32)],
        compiler_params=_cparams("parallel", "parallel", "arbitrary"),
        name="diff_attn",
    )(q, k, v, lamv, subln)


def _mem_kv_kernel(mem_ref, g_ref, w_ref, k_ref, v_ref):
    d = mem_ref.shape[-1]
    mn = _rms(mem_ref[0], g_ref[...]).astype(BF16)
    k_ref[0] = jnp.dot(mn, w_ref[:, :d], preferred_element_type=F32).astype(BF16)
    v_ref[0] = jnp.dot(mn, w_ref[:, d:], preferred_element_type=F32).astype(BF16)


def _mem_kv(mem, norm_mem, w_kv):
    b, m, d = mem.shape
    spec = pl.BlockSpec((1, m, d), lambda i: (i, 0, 0))
    sds = jax.ShapeDtypeStruct((b, m, d), BF16)
    return pl.pallas_call(
        _mem_kv_kernel,
        grid=(b,),
        in_specs=[spec, pl.BlockSpec(norm_mem.shape, lambda i: (0, 0)),
                  pl.BlockSpec(w_kv.shape, lambda i: (0, 0))],
        out_specs=[spec, spec],
        out_shape=[sds, sds],
        compiler_params=_cparams("parallel"),
        name="mem_kv",
    )(mem, norm_mem, w_kv)


def _postmix_kernel(x_ref, a_ref, o_ref, wout_ref, gq_ref, wq_ref, km_ref, vm_ref, wo_ref,
                    gf_ref, wrh_ref, wrl_ref, br_ref, x2_ref, h3_ref, route_ref):
    tm, d = x_ref.shape
    half = a_ref.shape[1]
    hd = d // MEM_HEADS
    mixed = (jnp.dot(a_ref[...], wout_ref[:half, :], preferred_element_type=F32)
             + jnp.dot(o_ref[...], wout_ref[half:, :], preferred_element_type=F32))
    x1 = x_ref[...] + mixed

    hq = _rms(x1, gq_ref[...]).astype(BF16)
    q = (jnp.dot(hq, wq_ref[...], preferred_element_type=F32) * (hd ** -0.5)).astype(BF16)
    heads = []
    for hh in range(MEM_HEADS):
        cs = slice(hh * hd, (hh + 1) * hd)
        s = lax.dot_general(q[:, cs], km_ref[0, :, cs], (((1,), (1,)), ((), ())),
                            preferred_element_type=F32)
        e = jnp.exp(s - jnp.max(s, axis=-1, keepdims=True))
        p = e / jnp.sum(e, axis=-1, keepdims=True)
        heads.append(jnp.dot(p.astype(BF16), vm_ref[0, :, cs], preferred_element_type=F32).astype(BF16))
    x2 = x1 + jnp.dot(jnp.concatenate(heads, axis=1), wo_ref[...], preferred_element_type=F32)
    x2_ref[...] = x2

    h3 = _rms(x2, gf_ref[...])
    h3_hi = h3.astype(BF16)
    h3_ref[...] = h3_hi
    h3_lo = (h3 - h3_hi.astype(F32)).astype(BF16)
    lg = (jnp.dot(h3_hi, wrh_ref[...], preferred_element_type=F32)
          + jnp.dot(h3_lo, wrh_ref[...], preferred_element_type=F32)
          + jnp.dot(h3_hi, wrl_ref[...], preferred_element_type=F32)) + br_ref[...]

    lane = lax.broadcasted_iota(jnp.int32, lg.shape, 1)
    lanef = lane.astype(F32)
    big = float(LANES)
    ninf = -jnp.inf
    gl = jnp.where(lane < N_GROUPS, lg, ninf)
    gmax = jnp.max(gl, axis=-1, keepdims=True)
    gidx = jnp.min(jnp.where(gl == gmax, lanef, big), axis=-1, keepdims=True)
    gate = 1.0 / jnp.sum(jnp.exp(gl - gmax), axis=-1, keepdims=True)
    lo = N_GROUPS + EXPERTS_PER_GROUP * gidx
    el = jnp.where((lanef >= lo) & (lanef < lo + EXPERTS_PER_GROUP), lg, ninf)
    e1 = jnp.max(el, axis=-1, keepdims=True)
    i1 = jnp.min(jnp.where(el == e1, lanef, big), axis=-1, keepdims=True)
    el2 = jnp.where(lanef == i1, ninf, el)
    e2 = jnp.max(el2, axis=-1, keepdims=True)
    i2 = jnp.min(jnp.where(el2 == e2, lanef, big), axis=-1, keepdims=True)
    tt = jnp.exp(e2 - e1)
    w1 = gate / (1.0 + tt)
    w2 = gate * tt / (1.0 + tt)
    route_ref[...] = jnp.where(lane == 0, i1 - N_GROUPS,
                               jnp.where(lane == 1, i2 - N_GROUPS,
                                         jnp.where(lane == 2, w1, jnp.where(lane == 3, w2, 0.0))))


def _postmix(x2d, a_out, o_attn, w_out, norm_xq, w_q, k_mem, v_mem, w_o, norm_ffn,
             wr_hi, wr_lo, b_r, seq):
    n, d = x2d.shape
    tm = TOKEN_TILE
    per_b = seq // tm
    half = a_out.shape[1]
    row = lambda w: pl.BlockSpec((tm, w), lambda i: (i, 0))
    full = lambda a: pl.BlockSpec(a.shape, lambda i: (0,) * a.ndim)
    memspec = pl.BlockSpec((1,) + k_mem.shape[1:], lambda i: (i // per_b, 0, 0))
    return pl.pallas_call(
        _postmix_kernel,
        grid=(n // tm,),
        in_specs=[row(d), row(half), row(half), full(w_out), full(norm_xq), full(w_q),
                  memspec, memspec, full(w_o), full(norm_ffn), full(wr_hi), full(wr_lo), full(b_r)],
        out_specs=[row(d), row(d), row(LANES)],
        out_shape=[jax.ShapeDtypeStruct((n, d), F32), jax.ShapeDtypeStruct((n, d), BF16),
                   jax.ShapeDtypeStruct((n, LANES), F32)],
        compiler_params=_cparams("parallel"),
        name="postmix",
    )(x2d, a_out, o_attn, w_out, norm_xq, w_q, k_mem, v_mem, w_o, norm_ffn, wr_hi, wr_lo, b_r)


def _expert_kernel(be_ref, nused_ref, x_ref, wg_ref, wu_ref, wd_ref, y_ref):
    i = pl.program_id(0)

    @pl.when(i < nused_ref[0])
    def _():
        x = x_ref[...]
        g = jnp.dot(x, wg_ref[0], preferred_element_type=F32)
        u = jnp.dot(x, wu_ref[0], preferred_element_type=F32)
        hmid = (jax.nn.silu(g) * u).astype(BF16)
        y_ref[...] = jnp.dot(hmid, wd_ref[0], preferred_element_type=F32).astype(y_ref.dtype)

    @pl.when(i >= nused_ref[0])
    def _():
        y_ref[...] = jnp.zeros(y_ref.shape, y_ref.dtype)


def _experts(block_expert, n_used, x_rows, w_gate, w_up, w_down):
    p, d = x_rows.shape
    tb = EXPERT_ROWS
    de = w_gate.shape[-1]
    grid_spec = pltpu.PrefetchScalarGridSpec(
        num_scalar_prefetch=2,
        grid=(p // tb,),
        in_specs=[pl.BlockSpec((tb, d), lambda i, be, nu: (i, 0)),
                  pl.BlockSpec((1, d, de), lambda i, be, nu: (be[i], 0, 0)),
                  pl.BlockSpec((1, d, de), lambda i, be, nu: (be[i], 0, 0)),
                  pl.BlockSpec((1, de, d), lambda i, be, nu: (be[i], 0, 0))],
        out_specs=pl.BlockSpec((tb, d), lambda i, be, nu: (i, 0)),
    )
    return pl.pallas_call(
        _expert_kernel,
        grid_spec=grid_spec,
        out_shape=jax.ShapeDtypeStruct((p, d), BF16),
        compiler_params=_cparams("arbitrary"),
        name="experts",
    )(block_expert, n_used, x_rows, w_gate, w_up, w_down)


def _combine_kernel(x2_ref, y_ref, route_ref, g_ref, o_ref):
    d = x2_ref.shape[1]
    r = route_ref[...]
    y = y_ref[...].astype(F32)
    x = x2_ref[...] + r[:, 2:3] * y[:, :d] + r[:, 3:4] * y[:, d:]
    o_ref[...] = _rms(x, g_ref[...])


def _combine(x2, y_pair, route, norm_final):
    n, d = x2.shape
    tm = TOKEN_TILE
    row = lambda w: pl.BlockSpec((tm, w), lambda i: (i, 0))
    return pl.pallas_call(
        _combine_kernel,
        grid=(n // tm,),
        in_specs=[row(d), row(TOP_K * d), row(LANES), pl.BlockSpec(norm_final.shape, lambda i: (0, 0))],
        out_specs=row(d),
        out_shape=jax.ShapeDtypeStruct((n, d), F32),
        compiler_params=_cparams("parallel"),
        name="combine",
    )(x2, y_pair, route, norm_final)


def _dispatch_plan(eid, n_rows):
    a = eid.shape[0]
    tb = EXPERT_ROWS
    onehot = (eid[:, None] == jnp.arange(N_EXPERTS, dtype=jnp.int32)[None, :]).astype(jnp.int32)
    csum = jnp.cumsum(onehot, axis=0)
    rank = jnp.sum(onehot * csum, axis=1) - 1
    counts = csum[-1]
    padded = (counts + tb - 1) // tb * tb
    pend = jnp.cumsum(padded)
    pstart = pend - padded
    pos = (pstart[eid] + rank).astype(jnp.int32)
    tok = jnp.arange(a, dtype=jnp.int32) // TOP_K
    row_tok = jnp.zeros((n_rows,), jnp.int32).at[pos].set(tok)
    nb = n_rows // tb
    block_expert = jnp.clip(jnp.searchsorted(pend, jnp.arange(nb, dtype=jnp.int32) * tb, side='right'),
                            0, N_EXPERTS - 1).astype(jnp.int32)
    n_used = (pend[-1] // tb).astype(jnp.int32).reshape(1)
    return pos, row_tok, block_expert, n_used


def kernel(x, mem, norm_mix, w_in, sgu_ln_g, sgu_ln_b, sgu_w, sgu_b, lambda_q1, lambda_k1, lambda_q2,
           lambda_k2, diff_subln, w_out, norm_xq, norm_mem, w_q_mem, w_kv_mem, w_o_mem, norm_ffn,
           w_router_group, b_router_group, w_router_expert, b_router_expert, w_gate, w_up, w_down,
           norm_final):
    bn, sn, d = x.shape
    n = bn * sn
    depth = w_in.shape[0]
    assert depth == 1 and sn % TOKEN_TILE == 0 and sn % ATTN_TILE == 0
    x2d = x.reshape(n, d)
    for l in range(depth):
        a_out, q, k, v = _inproj_sgu(x2d, norm_mix[l][None], w_in[l].astype(BF16), sgu_ln_g[l],
                                     sgu_ln_b[l], sgu_w[l], sgu_b[l].T)
        lamv = jnp.stack([lambda_q1[l], lambda_k1[l], lambda_q2[l], lambda_k2[l]])
        wide = q.shape[1]
        o_attn = _diff_attn(q.reshape(bn, sn, wide), k.reshape(bn, sn, wide), v.reshape(bn, sn, wide),
                            lamv, diff_subln[l][None]).reshape(n, wide)
        k_mem, v_mem = _mem_kv(mem, norm_mem[l][None], w_kv_mem[l].astype(BF16))

        w_r = jnp.concatenate([w_router_group[l], w_router_expert[l]], axis=1)
        w_r = jnp.pad(w_r, ((0, 0), (0, LANES - w_r.shape[1])))
        wr_hi = w_r.astype(BF16)
        wr_lo = (w_r - wr_hi.astype(F32)).astype(BF16)
        b_r = jnp.concatenate([b_router_group[l], b_router_expert[l]])
        b_r = jnp.pad(b_r, (0, LANES - b_r.shape[0]))[None]
        x2, h3, route = _postmix(x2d, a_out, o_attn, w_out[l].astype(BF16), norm_xq[l][None],
                                 w_q_mem[l].astype(BF16), k_mem, v_mem, w_o_mem[l].astype(BF16),
                                 norm_ffn[l][None], wr_hi, wr_lo, b_r, sn)

        eid = route[:, :TOP_K].astype(jnp.int32).reshape(-1)
        n_rows = n * TOP_K + N_EXPERTS * EXPERT_ROWS
        pos, row_tok, block_expert, n_used = _dispatch_plan(eid, n_rows)
        x_rows = jnp.take(h3, row_tok, axis=0)
        y_rows = _experts(block_expert, n_used, x_rows, w_gate[l].astype(BF16), w_up[l].astype(BF16),
                          w_down[l].astype(BF16))
        y_pair = jnp.take(y_rows, pos, axis=0).reshape(n, TOP_K * d)
        x2d = _combine(x2, y_pair, route, norm_final[None])
    return x2d.reshape(bn, sn, d)
```

```python
import math

import jax
import jax.numpy as jnp
from jax import lax
from jax.experimental import pallas as pl
from jax.experimental.pallas import tpu as pltpu

F32 = jnp.float32
BF16 = jnp.bfloat16

LANES = 128
SGU_GROUPS = 4
SGU_CH = 128
CHUNK = 128
DIFF_HEADS = 4
DIFF_QK_DIM = 64
DIFF_V_DIM = 128
MEM_HEADS = 4
N_GROUPS = 4
EXPERTS_PER_GROUP = 8
N_EXPERTS = N_GROUPS * EXPERTS_PER_GROUP
TOP_K = 2
RMS_EPS = 1e-6
LN_EPS = 1e-5
LOG2E = math.log2(math.e)
LAMBDA_INIT = 0.8 - 0.6 * math.exp(-0.3 * 0)

TOKEN_TILE = 512
ATTN_TQ = 512
ATTN_TK = 512
ONES_ROWS = 16
EXPERT_ROWS = 512
VMEM_LIMIT = 56 * 1024 * 1024


def _cparams(*sem):
    return pltpu.CompilerParams(dimension_semantics=sem, vmem_limit_bytes=VMEM_LIMIT)


def _gelu(x):
    return 0.5 * x * (1.0 + lax.erf(x * (2.0 ** -0.5)))


def _rms(x, g):
    ms = jnp.mean(x * x, axis=-1, keepdims=True)
    return x * lax.rsqrt(ms + RMS_EPS) * g


def _inproj_sgu_kernel(x_ref, g_ref, w_ref, wt_ref, lng_ref, lnb_ref, ws_ref, bs_ref,
                       a_ref, k_ref, qt_ref, vt_ref):
    tm = x_ref.shape[0]
    h = _rms(x_ref[...], g_ref[...]).astype(BF16)
    sw = SGU_GROUPS * SGU_CH
    qkw = DIFF_HEADS * 2 * DIFF_QK_DIM

    def proj(c0, c1):
        return jnp.dot(h, w_ref[:, c0:c1], preferred_element_type=F32)

    def proj_t(r0, r1):
        return lax.dot_general(wt_ref[r0:r1, :], h, (((1,), (1,)), ((), ())), preferred_element_type=F32)

    u = _gelu(proj(0, sw))
    v = _gelu(proj(sw, 2 * sw))
    k_ref[...] = proj(2 * sw, 2 * sw + qkw).astype(BF16)
    qt_ref[0] = (proj_t(0, qkw) * (DIFF_QK_DIM ** -0.5 * LOG2E)).astype(BF16)
    vt = proj_t(qkw, qkw + DIFF_HEADS * DIFF_V_DIM)
    ones = jnp.ones((ONES_ROWS, tm), F32)
    pieces = []
    for hh in range(DIFF_HEADS):
        pieces += [vt[hh * DIFF_V_DIM:(hh + 1) * DIFF_V_DIM, :], ones]
    vt_ref[0] = jnp.concatenate(pieces, axis=0).astype(BF16)

    row = lax.broadcasted_iota(jnp.int32, (CHUNK, CHUNK), 0)
    col = lax.broadcasted_iota(jnp.int32, (CHUNK, CHUNK), 1)
    nchunk = tm // CHUNK
    for g in range(SGU_GROUPS):
        cs = slice(g * SGU_CH, (g + 1) * SGU_CH)
        vg = v[:, cs]
        mu = jnp.mean(vg, axis=-1, keepdims=True)
        d = vg - mu
        var = jnp.mean(d * d, axis=-1, keepdims=True)
        vn = (d * lax.rsqrt(var + LN_EPS) * lng_ref[g:g + 1, :] + lnb_ref[g:g + 1, :]).astype(BF16)
        rhs = jnp.concatenate([vn[c * CHUNK:(c + 1) * CHUNK, :] for c in range(nchunk)], axis=1)
        wt = jnp.where(row >= col, ws_ref[g], 0.0).astype(BF16)
        sp = jnp.dot(wt, rhs, preferred_element_type=F32) + bs_ref[:, g:g + 1]
        for c in range(nchunk):
            rs = slice(c * CHUNK, (c + 1) * CHUNK)
            a_ref[rs, cs] = (u[rs, cs] * sp[:, c * SGU_CH:(c + 1) * SGU_CH]).astype(BF16)


def _inproj_sgu(x2d, norm_mix, w_uvk, w_qv_t, ln_g, ln_b, w_s, b_s_t, seq):
    n, d = x2d.shape
    tm = TOKEN_TILE
    per_b = seq // tm
    wide = DIFF_HEADS * DIFF_V_DIM
    row_spec = pl.BlockSpec((tm, wide), lambda i: (i, 0))
    vt_rows = DIFF_HEADS * (DIFF_V_DIM + ONES_ROWS)
    t_spec = lambda rows: pl.BlockSpec((1, rows, tm), lambda i: (i // per_b, 0, i % per_b))
    full = lambda a: pl.BlockSpec(a.shape, lambda i: (0,) * a.ndim)
    row_sds = jax.ShapeDtypeStruct((n, wide), BF16)
    t_sds = lambda rows: jax.ShapeDtypeStruct((n // seq, rows, seq), BF16)
    return pl.pallas_call(
        _inproj_sgu_kernel,
        grid=(n // tm,),
        in_specs=[pl.BlockSpec((tm, d), lambda i: (i, 0)), full(norm_mix), full(w_uvk), full(w_qv_t),
                  full(ln_g), full(ln_b), full(w_s), full(b_s_t)],
        out_specs=[row_spec, row_spec, t_spec(wide), t_spec(vt_rows)],
        out_shape=[row_sds, row_sds, t_sds(wide), t_sds(vt_rows)],
        compiler_params=_cparams("parallel"),
        name="inproj_sgu",
    )(x2d, norm_mix, w_uvk, w_qv_t, ln_g, ln_b, w_s, b_s_t)


def _diff_attn_kernel(qt_ref, k_ref, vt_ref, lamv_ref, sub_ref, o_ref, kb_scr, acc_scr, s0_scr, s1_scr):
    dqk = DIFF_QK_DIM
    tq = qt_ref.shape[2]
    tk = kb_scr.shape[1]
    h = pl.program_id(1)
    iq = pl.program_id(2)
    slope = jnp.exp2(-(8.0 / DIFF_HEADS) * jnp.full((1, 1), h + 1, jnp.int32).astype(F32))

    qt = qt_ref[0]
    frow = lax.broadcasted_iota(jnp.int32, qt.shape, 0)
    zero = jnp.zeros_like(qt)
    qw = jnp.concatenate([jnp.where(frow < dqk, qt, zero), jnp.where(frow >= dqk, qt, zero)], axis=1)

    slope2 = slope * LOG2E
    nvar = tk // tq

    @pl.when(iq == 0)
    def _():
        krow = lax.broadcasted_iota(jnp.int32, (tk, 2 * tq), 0)
        qcol = lax.broadcasted_iota(jnp.int32, (tk, 2 * tq), 1) % tq
        kb = slope2 * krow.astype(F32)
        kb_scr[0] = kb
        for v in range(nvar):
            kb_scr[1 + v] = jnp.where(krow - v * tq <= qcol, kb, -jnp.inf)
        kb_scr[1 + nvar] = jnp.full((tk, 2 * tq), -jnp.inf, F32)

    acc_scr[...] = jnp.zeros(acc_scr.shape, F32)
    q0 = iq * tq
    n_full = q0 // tk
    last_bias = 1 + iq % nvar
    n_pairs = (n_full + 2) // 2
    max_blk = k_ref.shape[1] // tk - 1

    def block_offset(j):
        off = pl.multiple_of(jnp.minimum(j, max_blk) * tk, tk)
        return off, slope2 * (off - q0).astype(F32)

    def scores(j, buf):
        off, cj = block_offset(j)
        sel = jnp.where(j < n_full, 0, jnp.where(j == n_full, last_bias, 1 + nvar))
        s = jnp.dot(k_ref[0, pl.ds(off, tk), :], qw, preferred_element_type=F32) + kb_scr[sel]
        buf[...] = s
        return jnp.max(s, axis=0, keepdims=True) + cj

    def accumulate(j, buf, bm, m_prev):
        off, cj = block_offset(j)
        m_new = jnp.maximum(m_prev, bm)
        p = jnp.exp2(buf[...] - (m_new - cj))
        alpha = jnp.exp2(m_prev - m_new)
        pv = jnp.dot(vt_ref[0, :, pl.ds(off, tk)], p.astype(BF16), preferred_element_type=F32)
        acc_scr[...] = alpha * acc_scr[...] + pv
        return m_new

    def body(t, c):
        m, bm0 = c
        bm1 = scores(2 * t + 1, s1_scr)
        m = accumulate(2 * t, s0_scr, bm0, m)
        bm0 = scores(2 * t + 2, s0_scr)
        m = accumulate(2 * t + 1, s1_scr, bm1, m)
        return m, bm0

    m0 = jnp.full((1, 2 * tq), -jnp.inf, F32)
    bm0 = scores(0, s0_scr)
    m, bm0 = lax.fori_loop(0, n_pairs - 1, body, (m0, bm0))
    last = 2 * n_pairs - 1
    bm1 = scores(last, s1_scr)
    m = accumulate(last - 1, s0_scr, bm0, m)
    m = accumulate(last, s1_scr, bm1, m)

    lam = (jnp.exp(jnp.sum(lamv_ref[0:1, :] * lamv_ref[1:2, :], axis=-1, keepdims=True))
           - jnp.exp(jnp.sum(lamv_ref[2:3, :] * lamv_ref[3:4, :], axis=-1, keepdims=True))
           + LAMBDA_INIT)
    dv = DIFF_V_DIM
    on = acc_scr[:dv, :] * (1.0 / acc_scr[dv:dv + 1, :])
    o = on[:, :tq] - lam * on[:, tq:]
    ms = jnp.mean(o * o, axis=0, keepdims=True)
    o = o * lax.rsqrt(ms + RMS_EPS) * sub_ref[...] * (1.0 - LAMBDA_INIT)
    o_ref[0] = o.T.astype(o_ref.dtype)


def _diff_attn(qt, k, vt, lamv, subln_col):
    b, s, _ = k.shape
    tq, tk = ATTN_TQ, ATTN_TK
    dv = DIFF_V_DIM
    assert tk % tq == 0 and s % tk == 0
    return pl.pallas_call(
        _diff_attn_kernel,
        grid=(b, DIFF_HEADS, s // tq),
        in_specs=[pl.BlockSpec((1, 2 * DIFF_QK_DIM, tq), lambda bi, hi, qi: (bi, hi, qi)),
                  pl.BlockSpec((1, s, 2 * DIFF_QK_DIM), lambda bi, hi, qi: (bi, 0, hi)),
                  pl.BlockSpec((1, dv + ONES_ROWS, s), lambda bi, hi, qi: (bi, hi, 0)),
                  pl.BlockSpec(lamv.shape, lambda bi, hi, qi: (0, 0)),
                  pl.BlockSpec(subln_col.shape, lambda bi, hi, qi: (0, 0))],
        out_specs=pl.BlockSpec((1, tq, dv), lambda bi, hi, qi: (bi, qi, hi)),
        out_shape=jax.ShapeDtypeStruct((b, s, DIFF_HEADS * dv), BF16),
        scratch_shapes=[pltpu.VMEM((2 + tk // tq, tk, 2 * tq), F32),
                        pltpu.VMEM((dv + ONES_ROWS, 2 * tq), F32),
                        pltpu.VMEM((tk, 2 * tq), F32), pltpu.VMEM((tk, 2 * tq), F32)],
        compiler_params=_cparams("arbitrary", "arbitrary", "arbitrary"),
        name="diff_attn",
    )(qt, k, vt, lamv, subln_col)


def _mem_kv_kernel(mem_ref, g_ref, w_ref, k_ref, v_ref):
    d = mem_ref.shape[-1]
    mn = _rms(mem_ref[0], g_ref[...]).astype(BF16)
    k_ref[0] = jnp.dot(mn, w_ref[:, :d], preferred_element_type=F32).astype(BF16)
    v_ref[0] = jnp.dot(mn, w_ref[:, d:], preferred_element_type=F32).astype(BF16)


def _mem_kv(mem, norm_mem, w_kv):
    b, m, d = mem.shape
    spec = pl.BlockSpec((1, m, d), lambda i: (i, 0, 0))
    sds = jax.ShapeDtypeStruct((b, m, d), BF16)
    return pl.pallas_call(
        _mem_kv_kernel,
        grid=(b,),
        in_specs=[spec, pl.BlockSpec(norm_mem.shape, lambda i: (0, 0)),
                  pl.BlockSpec(w_kv.shape, lambda i: (0, 0))],
        out_specs=[spec, spec],
        out_shape=[sds, sds],
        compiler_params=_cparams("parallel"),
        name="mem_kv",
    )(mem, norm_mem, w_kv)


def _postmix_kernel(x_ref, a_ref, o_ref, wout_ref, gq_ref, wq_ref, km_ref, vm_ref, wo_ref,
                    gf_ref, wrh_ref, wrl_ref, br_ref, x2_ref, h3_ref, route_ref):
    tm, d = x_ref.shape
    half = a_ref.shape[1]
    hd = d // MEM_HEADS
    mixed = (jnp.dot(a_ref[...], wout_ref[:half, :], preferred_element_type=F32)
             + jnp.dot(o_ref[...], wout_ref[half:, :], preferred_element_type=F32))
    x1 = x_ref[...] + mixed

    hq = _rms(x1, gq_ref[...]).astype(BF16)
    q = (jnp.dot(hq, wq_ref[...], preferred_element_type=F32) * (hd ** -0.5)).astype(BF16)
    heads = []
    for hh in range(MEM_HEADS):
        cs = slice(hh * hd, (hh + 1) * hd)
        s = lax.dot_general(q[:, cs], km_ref[0, :, cs], (((1,), (1,)), ((), ())),
                            preferred_element_type=F32)
        e = jnp.exp(s - jnp.max(s, axis=-1, keepdims=True))
        p = e / jnp.sum(e, axis=-1, keepdims=True)
        heads.append(jnp.dot(p.astype(BF16), vm_ref[0, :, cs], preferred_element_type=F32).astype(BF16))
    x2 = x1 + jnp.dot(jnp.concatenate(heads, axis=1), wo_ref[...], preferred_element_type=F32)
    x2_ref[...] = x2

    h3 = _rms(x2, gf_ref[...])
    h3_hi = h3.astype(BF16)
    h3_ref[...] = h3_hi
    h3_lo = (h3 - h3_hi.astype(F32)).astype(BF16)
    lg = (jnp.dot(h3_hi, wrh_ref[...], preferred_element_type=F32)
          + jnp.dot(h3_lo, wrh_ref[...], preferred_element_type=F32)
          + jnp.dot(h3_hi, wrl_ref[...], preferred_element_type=F32)) + br_ref[...]

    lane = lax.broadcasted_iota(jnp.int32, lg.shape, 1)
    lanef = lane.astype(F32)
    big = float(LANES)
    ninf = -jnp.inf
    gl = jnp.where(lane < N_GROUPS, lg, ninf)
    gmax = jnp.max(gl, axis=-1, keepdims=True)
    gidx = jnp.min(jnp.where(gl == gmax, lanef, big), axis=-1, keepdims=True)
    gate = 1.0 / jnp.sum(jnp.exp(gl - gmax), axis=-1, keepdims=True)
    lo = N_GROUPS + EXPERTS_PER_GROUP * gidx
    el = jnp.where((lanef >= lo) & (lanef < lo + EXPERTS_PER_GROUP), lg, ninf)
    e1 = jnp.max(el, axis=-1, keepdims=True)
    i1 = jnp.min(jnp.where(el == e1, lanef, big), axis=-1, keepdims=True)
    el2 = jnp.where(lanef == i1, ninf, el)
    e2 = jnp.max(el2, axis=-1, keepdims=True)
    i2 = jnp.min(jnp.where(el2 == e2, lanef, big), axis=-1, keepdims=True)
    tt = jnp.exp(e2 - e1)
    w1 = gate / (1.0 + tt)
    w2 = gate * tt / (1.0 + tt)
    route_ref[...] = jnp.where(lane == 0, i1 - N_GROUPS,
                               jnp.where(lane == 1, i2 - N_GROUPS,
                                         jnp.where(lane == 2, w1, jnp.where(lane == 3, w2, 0.0))))


def _postmix(x2d, a_out, o_attn, w_out, norm_xq, w_q, k_mem, v_mem, w_o, norm_ffn,
             wr_hi, wr_lo, b_r, seq):
    n, d = x2d.shape
    tm = TOKEN_TILE
    per_b = seq // tm
    half = a_out.shape[1]
    row = lambda w: pl.BlockSpec((tm, w), lambda i: (i, 0))
    full = lambda a: pl.BlockSpec(a.shape, lambda i: (0,) * a.ndim)
    memspec = pl.BlockSpec((1,) + k_mem.shape[1:], lambda i: (i // per_b, 0, 0))
    return pl.pallas_call(
        _postmix_kernel,
        grid=(n // tm,),
        in_specs=[row(d), row(half), row(half), full(w_out), full(norm_xq), full(w_q),
                  memspec, memspec, full(w_o), full(norm_ffn), full(wr_hi), full(wr_lo), full(b_r)],
        out_specs=[row(d), row(d), row(LANES)],
        out_shape=[jax.ShapeDtypeStruct((n, d), F32), jax.ShapeDtypeStruct((n, d), BF16),
                   jax.ShapeDtypeStruct((n, LANES), F32)],
        compiler_params=_cparams("parallel"),
        name="postmix",
    )(x2d, a_out, o_attn, w_out, norm_xq, w_q, k_mem, v_mem, w_o, norm_ffn, wr_hi, wr_lo, b_r)


def _expert_kernel(be_ref, nused_ref, x_ref, wg_ref, wu_ref, wd_ref, y_ref):
    i = pl.program_id(0)

    @pl.when(i < nused_ref[0])
    def _():
        x = x_ref[...]
        g = jnp.dot(x, wg_ref[0], preferred_element_type=F32)
        u = jnp.dot(x, wu_ref[0], preferred_element_type=F32)
        hmid = (jax.nn.silu(g) * u).astype(BF16)
        y_ref[...] = jnp.dot(hmid, wd_ref[0], preferred_element_type=F32).astype(y_ref.dtype)

    @pl.when(i >= nused_ref[0])
    def _():
        y_ref[...] = jnp.zeros(y_ref.shape, y_ref.dtype)


def _experts(block_expert, n_used, x_rows, w_gate, w_up, w_down):
    p, d = x_rows.shape
    tb = EXPERT_ROWS
    de = w_gate.shape[-1]
    grid_spec = pltpu.PrefetchScalarGridSpec(
        num_scalar_prefetch=2,
        grid=(p // tb,),
        in_specs=[pl.BlockSpec((tb, d), lambda i, be, nu: (i, 0)),
                  pl.BlockSpec((1, d, de), lambda i, be, nu: (be[i], 0, 0)),
                  pl.BlockSpec((1, d, de), lambda i, be, nu: (be[i], 0, 0)),
                  pl.BlockSpec((1, de, d), lambda i, be, nu: (be[i], 0, 0))],
        out_specs=pl.BlockSpec((tb, d), lambda i, be, nu: (i, 0)),
    )
    return pl.pallas_call(
        _expert_kernel,
        grid_spec=grid_spec,
        out_shape=jax.ShapeDtypeStruct((p, d), BF16),
        compiler_params=_cparams("arbitrary"),
        name="experts",
    )(block_expert, n_used, x_rows, w_gate, w_up, w_down)


def _combine_kernel(x2_ref, y0_ref, y1_ref, route_ref, g_ref, o_ref):
    r = route_ref[...]
    x = x2_ref[...] + r[:, 2:3] * y0_ref[...].astype(F32) + r[:, 3:4] * y1_ref[...].astype(F32)
    o_ref[...] = _rms(x, g_ref[...])


def _combine(x2, y0, y1, route, norm_final):
    n, d = x2.shape
    tm = TOKEN_TILE
    row = lambda w: pl.BlockSpec((tm, w), lambda i: (i, 0))
    return pl.pallas_call(
        _combine_kernel,
        grid=(n // tm,),
        in_specs=[row(d), row(d), row(d), row(LANES), pl.BlockSpec(norm_final.shape, lambda i: (0, 0))],
        out_specs=row(d),
        out_shape=jax.ShapeDtypeStruct((n, d), F32),
        compiler_params=_cparams("parallel"),
        name="combine",
    )(x2, y0, y1, route, norm_final)


def _dispatch_plan(eid, n_rows):
    a = eid.shape[0]
    tb = EXPERT_ROWS
    onehot = (eid[:, None] == jnp.arange(N_EXPERTS, dtype=jnp.int32)[None, :]).astype(jnp.int32)
    csum = jnp.cumsum(onehot, axis=0)
    rank = jnp.sum(onehot * csum, axis=1) - 1
    counts = csum[-1]
    padded = (counts + tb - 1) // tb * tb
    pend = jnp.cumsum(padded)
    pstart = pend - padded
    pos = (pstart[eid] + rank).astype(jnp.int32)
    tok = jnp.arange(a, dtype=jnp.int32) // TOP_K
    row_tok = jnp.zeros((n_rows,), jnp.int32).at[pos].set(tok)
    nb = n_rows // tb
    block_expert = jnp.clip(jnp.searchsorted(pend, jnp.arange(nb, dtype=jnp.int32) * tb, side='right'),
                            0, N_EXPERTS - 1).astype(jnp.int32)
    n_used = (pend[-1] // tb).astype(jnp.int32).reshape(1)
    return pos, row_tok, block_expert, n_used


def kernel(x, mem, norm_mix, w_in, sgu_ln_g, sgu_ln_b, sgu_w, sgu_b, lambda_q1, lambda_k1, lambda_q2,
           lambda_k2, diff_subln, w_out, norm_xq, norm_mem, w_q_mem, w_kv_mem, w_o_mem, norm_ffn,
           w_router_group, b_router_group, w_router_expert, b_router_expert, w_gate, w_up, w_down,
           norm_final):
    bn, sn, d = x.shape
    n = bn * sn
    assert w_in.shape[0] == 1 and sn % TOKEN_TILE == 0 and sn % ATTN_TK == 0
    l = 0
    x2d = x.reshape(n, d)

    sw = SGU_GROUPS * SGU_CH
    qkw = DIFF_HEADS * 2 * DIFF_QK_DIM
    w_in_l = w_in[l].astype(BF16)
    w_uvk = jnp.concatenate([w_in_l[:, :2 * sw], w_in_l[:, 2 * sw + qkw:2 * sw + 2 * qkw]], axis=1)
    w_qv_t = jnp.concatenate([w_in_l[:, 2 * sw:2 * sw + qkw], w_in_l[:, 2 * sw + 2 * qkw:]], axis=1).T
    a_out, k, qt, vt = _inproj_sgu(x2d, norm_mix[l][None], w_uvk, w_qv_t, sgu_ln_g[l], sgu_ln_b[l],
                                   sgu_w[l], sgu_b[l].T, sn)
    lamv = jnp.stack([lambda_q1[l], lambda_k1[l], lambda_q2[l], lambda_k2[l]])
    wide = k.shape[1]
    o_attn = _diff_attn(qt, k.reshape(bn, sn, wide), vt, lamv, diff_subln[l][:, None]).reshape(n, wide)
    k_mem, v_mem = _mem_kv(mem, norm_mem[l][None], w_kv_mem[l].astype(BF16))

    w_r = jnp.concatenate([w_router_group[l], w_router_expert[l]], axis=1)
    w_r = jnp.pad(w_r, ((0, 0), (0, LANES - w_r.shape[1])))
    wr_hi = w_r.astype(BF16)
    wr_lo = (w_r - wr_hi.astype(F32)).astype(BF16)
    b_r = jnp.concatenate([b_router_group[l], b_router_expert[l]])
    b_r = jnp.pad(b_r, (0, LANES - b_r.shape[0]))[None]
    x2, h3, route = _postmix(x2d, a_out, o_attn, w_out[l].astype(BF16), norm_xq[l][None],
                             w_q_mem[l].astype(BF16), k_mem, v_mem, w_o_mem[l].astype(BF16),
                             norm_ffn[l][None], wr_hi, wr_lo, b_r, sn)

    eid = route[:, :TOP_K].astype(jnp.int32).reshape(-1)
    n_rows = n * TOP_K + N_EXPERTS * EXPERT_ROWS
    pos, row_tok, block_expert, n_used = _dispatch_plan(eid, n_rows)
    x_rows = jnp.take(h3, row_tok, axis=0)
    y_rows = _experts(block_expert, n_used, x_rows, w_gate[l].astype(BF16), w_up[l].astype(BF16),
                      w_down[l].astype(BF16))
    pos_nk = pos.reshape(n, TOP_K)
    y0 = jnp.take(y_rows, pos_nk[:, 0], axis=0)
    y1 = jnp.take(y_rows, pos_nk[:, 1], axis=0)
    out = _combine(x2, y0, y1, route, norm_final[None])
    return out.reshape(bn, sn, d)
```

```python
import math

import jax
import jax.numpy as jnp
from jax import lax
from jax.experimental import pallas as pl
from jax.experimental.pallas import tpu as pltpu

F32 = jnp.float32
BF16 = jnp.bfloat16

LANES = 128
SGU_GROUPS = 4
SGU_CH = 128
CHUNK = 128
DIFF_HEADS = 4
DIFF_QK_DIM = 64
DIFF_V_DIM = 128
MEM_HEADS = 4
N_GROUPS = 4
EXPERTS_PER_GROUP = 8
N_EXPERTS = N_GROUPS * EXPERTS_PER_GROUP
TOP_K = 2
RMS_EPS = 1e-6
LN_EPS = 1e-5
LOG2E = math.log2(math.e)
LAMBDA_INIT = 0.8 - 0.6 * math.exp(-0.3 * 0)

TOKEN_TILE = 512
ATTN_TQ = 512
ATTN_TK = 512
ONES_ROWS = 16
EXPERT_ROWS = 512
VMEM_LIMIT = 56 * 1024 * 1024


def _cparams(*sem):
    return pltpu.CompilerParams(dimension_semantics=sem, vmem_limit_bytes=VMEM_LIMIT)


def _gelu(x):
    return 0.5 * x * (1.0 + lax.erf(x * (2.0 ** -0.5)))


def _rms(x, g):
    ms = jnp.mean(x * x, axis=-1, keepdims=True)
    return x * lax.rsqrt(ms + RMS_EPS) * g


def _inproj_sgu_kernel(x_ref, g_ref, w_ref, wt_ref, lng_ref, lnb_ref, ws_ref, bs_ref,
                       a_ref, k_ref, qt_ref, vt_ref):
    tm = x_ref.shape[0]
    h = _rms(x_ref[...], g_ref[...]).astype(BF16)
    sw = SGU_GROUPS * SGU_CH
    qkw = DIFF_HEADS * 2 * DIFF_QK_DIM

    def proj(c0, c1):
        return jnp.dot(h, w_ref[:, c0:c1], preferred_element_type=F32)

    def proj_t(r0, r1):
        return lax.dot_general(wt_ref[r0:r1, :], h, (((1,), (1,)), ((), ())), preferred_element_type=F32)

    u = _gelu(proj(0, sw))
    v = _gelu(proj(sw, 2 * sw))
    k_ref[...] = proj(2 * sw, 2 * sw + qkw).astype(BF16)
    qt_ref[0] = (proj_t(0, qkw) * (DIFF_QK_DIM ** -0.5 * LOG2E)).astype(BF16)
    vt = proj_t(qkw, qkw + DIFF_HEADS * DIFF_V_DIM)
    ones = jnp.ones((ONES_ROWS, tm), F32)
    pieces = []
    for hh in range(DIFF_HEADS):
        pieces += [vt[hh * DIFF_V_DIM:(hh + 1) * DIFF_V_DIM, :], ones]
    vt_ref[0] = jnp.concatenate(pieces, axis=0).astype(BF16)

    row = lax.broadcasted_iota(jnp.int32, (CHUNK, CHUNK), 0)
    col = lax.broadcasted_iota(jnp.int32, (CHUNK, CHUNK), 1)
    nchunk = tm // CHUNK
    for g in range(SGU_GROUPS):
        cs = slice(g * SGU_CH, (g + 1) * SGU_CH)
        vg = v[:, cs]
        mu = jnp.mean(vg, axis=-1, keepdims=True)
        d = vg - mu
        var = jnp.mean(d * d, axis=-1, keepdims=True)
        vn = (d * lax.rsqrt(var + LN_EPS) * lng_ref[g:g + 1, :] + lnb_ref[g:g + 1, :]).astype(BF16)
        rhs = jnp.concatenate([vn[c * CHUNK:(c + 1) * CHUNK, :] for c in range(nchunk)], axis=1)
        wt = jnp.where(row >= col, ws_ref[g], 0.0).astype(BF16)
        sp = jnp.dot(wt, rhs, preferred_element_type=F32) + bs_ref[:, g:g + 1]
        for c in range(nchunk):
            rs = slice(c * CHUNK, (c + 1) * CHUNK)
            a_ref[rs, cs] = (u[rs, cs] * sp[:, c * SGU_CH:(c + 1) * SGU_CH]).astype(BF16)


def _inproj_sgu(x2d, norm_mix, w_uvk, w_qv_t, ln_g, ln_b, w_s, b_s_t, seq):
    n, d = x2d.shape
    tm = TOKEN_TILE
    per_b = seq // tm
    wide = DIFF_HEADS * DIFF_V_DIM
    row_spec = pl.BlockSpec((tm, wide), lambda i: (i, 0))
    vt_rows = DIFF_HEADS * (DIFF_V_DIM + ONES_ROWS)
    t_spec = lambda rows: pl.BlockSpec((1, rows, tm), lambda i: (i // per_b, 0, i % per_b))
    full = lambda a: pl.BlockSpec(a.shape, lambda i: (0,) * a.ndim)
    row_sds = jax.ShapeDtypeStruct((n, wide), BF16)
    t_sds = lambda rows: jax.ShapeDtypeStruct((n // seq, rows, seq), BF16)
    return pl.pallas_call(
        _inproj_sgu_kernel,
        grid=(n // tm,),
        in_specs=[pl.BlockSpec((tm, d), lambda i: (i, 0)), full(norm_mix), full(w_uvk), full(w_qv_t),
                  full(ln_g), full(ln_b), full(w_s), full(b_s_t)],
        out_specs=[row_spec, row_spec, t_spec(wide), t_spec(vt_rows)],
        out_shape=[row_sds, row_sds, t_sds(wide), t_sds(vt_rows)],
        compiler_params=_cparams("parallel"),
        name="inproj_sgu",
    )(x2d, norm_mix, w_uvk, w_qv_t, ln_g, ln_b, w_s, b_s_t)


def _diff_attn_kernel(qt_ref, k_ref, vt_ref, lamv_ref, sub_ref, o_ref, kb_scr, acc_scr, s0_scr, s1_scr):
    dqk = DIFF_QK_DIM
    tq = qt_ref.shape[2]
    tk = kb_scr.shape[1]
    h = pl.program_id(1)
    iq = pl.program_id(2)
    slope = jnp.exp2(-(8.0 / DIFF_HEADS) * jnp.full((1, 1), h + 1, jnp.int32).astype(F32))

    qt = qt_ref[0]
    frow = lax.broadcasted_iota(jnp.int32, qt.shape, 0)
    zero = jnp.zeros_like(qt)
    qw = jnp.concatenate([jnp.where(frow < dqk, qt, zero), jnp.where(frow >= dqk, qt, zero)], axis=1)

    slope2 = slope * LOG2E
    nvar = tk // tq

    @pl.when(iq == 0)
    def _():
        krow = lax.broadcasted_iota(jnp.int32, (tk, 2 * tq), 0)
        qcol = lax.broadcasted_iota(jnp.int32, (tk, 2 * tq), 1) % tq
        kb = slope2 * krow.astype(F32)
        kb_scr[0] = kb
        for v in range(nvar):
            kb_scr[1 + v] = jnp.where(krow - v * tq <= qcol, kb, -jnp.inf)
        kb_scr[1 + nvar] = jnp.full((tk, 2 * tq), -jnp.inf, F32)

    acc_scr[...] = jnp.zeros(acc_scr.shape, F32)
    q0 = iq * tq
    n_full = q0 // tk
    last_bias = 1 + iq % nvar
    n_pairs = (n_full + 2) // 2
    max_blk = k_ref.shape[1] // tk - 1

    def block_offset(j):
        off = pl.multiple_of(jnp.minimum(j, max_blk) * tk, tk)
        return off, slope2 * (off - q0).astype(F32)

    def scores(j, buf):
        off, cj = block_offset(j)
        sel = jnp.where(j < n_full, 0, jnp.where(j == n_full, last_bias, 1 + nvar))
        s = jnp.dot(k_ref[0, pl.ds(off, tk), :], qw, preferred_element_type=F32) + kb_scr[sel]
        buf[...] = s
        return jnp.max(s, axis=0, keepdims=True) + cj

    def accumulate(j, buf, bm, m_prev):
        off, cj = block_offset(j)
        m_new = jnp.maximum(m_prev, bm)
        p = jnp.exp2(buf[...] - (m_new - cj))
        alpha = jnp.exp2(m_prev - m_new)
        pv = jnp.dot(vt_ref[0, :, pl.ds(off, tk)], p.astype(BF16), preferred_element_type=F32)
        acc_scr[...] = alpha * acc_scr[...] + pv
        return m_new

    def body(t, c):
        m, bm0 = c
        bm1 = scores(2 * t + 1, s1_scr)
        m = accumulate(2 * t, s0_scr, bm0, m)
        bm0 = scores(2 * t + 2, s0_scr)
        m = accumulate(2 * t + 1, s1_scr, bm1, m)
        return m, bm0

    m0 = jnp.full((1, 2 * tq), -jnp.inf, F32)
    bm0 = scores(0, s0_scr)
    m, bm0 = lax.fori_loop(0, n_pairs - 1, body, (m0, bm0))
    last = 2 * n_pairs - 1
    bm1 = scores(last, s1_scr)
    m = accumulate(last - 1, s0_scr, bm0, m)
    m = accumulate(last, s1_scr, bm1, m)

    lam = (jnp.exp(jnp.sum(lamv_ref[0:1, :] * lamv_ref[1:2, :], axis=-1, keepdims=True))
           - jnp.exp(jnp.sum(lamv_ref[2:3, :] * lamv_ref[3:4, :], axis=-1, keepdims=True))
           + LAMBDA_INIT)
    dv = DIFF_V_DIM
    on = acc_scr[:dv, :] * (1.0 / acc_scr[dv:dv + 1, :])
    o = on[:, :tq] - lam * on[:, tq:]
    ms = jnp.mean(o * o, axis=0, keepdims=True)
    o = o * lax.rsqrt(ms + RMS_EPS) * sub_ref[...] * (1.0 - LAMBDA_INIT)
    o_ref[0] = o.T.astype(o_ref.dtype)


def _diff_attn(qt, k, vt, lamv, subln_col):
    b, s, _ = k.shape
    tq, tk = ATTN_TQ, ATTN_TK
    dv = DIFF_V_DIM
    assert tk % tq == 0 and s % tk == 0
    return pl.pallas_call(
        _diff_attn_kernel,
        grid=(b, DIFF_HEADS, s // tq),
        in_specs=[pl.BlockSpec((1, 2 * DIFF_QK_DIM, tq), lambda bi, hi, qi: (bi, hi, qi)),
                  pl.BlockSpec((1, s, 2 * DIFF_QK_DIM), lambda bi, hi, qi: (bi, 0, hi)),
                  pl.BlockSpec((1, dv + ONES_ROWS, s), lambda bi, hi, qi: (bi, hi, 0)),
                  pl.BlockSpec(lamv.shape, lambda bi, hi, qi: (0, 0)),
                  pl.BlockSpec(subln_col.shape, lambda bi, hi, qi: (0, 0))],
        out_specs=pl.BlockSpec((1, tq, dv), lambda bi, hi, qi: (bi, qi, hi)),
        out_shape=jax.ShapeDtypeStruct((b, s, DIFF_HEADS * dv), BF16),
        scratch_shapes=[pltpu.VMEM((2 + tk // tq, tk, 2 * tq), F32),
                        pltpu.VMEM((dv + ONES_ROWS, 2 * tq), F32),
                        pltpu.VMEM((tk, 2 * tq), F32), pltpu.VMEM((tk, 2 * tq), F32)],
        compiler_params=_cparams("arbitrary", "arbitrary", "arbitrary"),
        name="diff_attn",
    )(qt, k, vt, lamv, subln_col)


def _mem_kv_kernel(mem_ref, g_ref, w_ref, k_ref, v_ref):
    d = mem_ref.shape[-1]
    mn = _rms(mem_ref[0], g_ref[...]).astype(BF16)
    k_ref[0] = jnp.dot(mn, w_ref[:, :d], preferred_element_type=F32).astype(BF16)
    v_ref[0] = jnp.dot(mn, w_ref[:, d:], preferred_element_type=F32).astype(BF16)


def _mem_kv(mem, norm_mem, w_kv):
    b, m, d = mem.shape
    spec = pl.BlockSpec((1, m, d), lambda i: (i, 0, 0))
    sds = jax.ShapeDtypeStruct((b, m, d), BF16)
    return pl.pallas_call(
        _mem_kv_kernel,
        grid=(b,),
        in_specs=[spec, pl.BlockSpec(norm_mem.shape, lambda i: (0, 0)),
                  pl.BlockSpec(w_kv.shape, lambda i: (0, 0))],
        out_specs=[spec, spec],
        out_shape=[sds, sds],
        compiler_params=_cparams("parallel"),
        name="mem_kv",
    )(mem, norm_mem, w_kv)


def _pack_rows(x):
    w = x.shape[1] // 2
    bits = pltpu.bitcast(x.astype(BF16).astype(F32), jnp.uint32)
    return (bits[:, :w] >> 16) | bits[:, w:]


def _unpack_rows(u):
    lo = pltpu.bitcast(u << 16, F32).astype(BF16)
    hi = pltpu.bitcast(u & jnp.uint32(0xFFFF0000), F32).astype(BF16)
    return lo, hi


def _postmix_kernel(x_ref, a_ref, o_ref, wout_ref, gq_ref, wq_ref, km_ref, vm_ref, wo_ref,
                    gf_ref, wrh_ref, wrl_ref, br_ref, x2_ref, h3_ref, route_ref, counts_ref,
                    ltri_scr, run_scr):
    tm, d = x_ref.shape

    @pl.when(pl.program_id(0) == 0)
    def _():
        r = lax.broadcasted_iota(jnp.int32, (tm, tm), 0)
        c = lax.broadcasted_iota(jnp.int32, (tm, tm), 1)
        ltri_scr[...] = jnp.where(c < r, 1.0, 0.0).astype(BF16)
        run_scr[...] = jnp.zeros(run_scr.shape, F32)

    half = a_ref.shape[1]
    hd = d // MEM_HEADS
    mixed = (jnp.dot(a_ref[...], wout_ref[:half, :], preferred_element_type=F32)
             + jnp.dot(o_ref[...], wout_ref[half:, :], preferred_element_type=F32))
    x1 = x_ref[...] + mixed

    hq = _rms(x1, gq_ref[...]).astype(BF16)
    q = (jnp.dot(hq, wq_ref[...], preferred_element_type=F32) * (hd ** -0.5)).astype(BF16)
    heads = []
    for hh in range(MEM_HEADS):
        cs = slice(hh * hd, (hh + 1) * hd)
        s = lax.dot_general(q[:, cs], km_ref[0, :, cs], (((1,), (1,)), ((), ())),
                            preferred_element_type=F32)
        e = jnp.exp(s - jnp.max(s, axis=-1, keepdims=True))
        p = e / jnp.sum(e, axis=-1, keepdims=True)
        heads.append(jnp.dot(p.astype(BF16), vm_ref[0, :, cs], preferred_element_type=F32).astype(BF16))
    x2 = x1 + jnp.dot(jnp.concatenate(heads, axis=1), wo_ref[...], preferred_element_type=F32)
    x2_ref[...] = x2

    h3 = _rms(x2, gf_ref[...])
    h3_hi = h3.astype(BF16)
    h3_ref[...] = _pack_rows(h3)
    h3_lo = (h3 - h3_hi.astype(F32)).astype(BF16)
    lg = (jnp.dot(h3_hi, wrh_ref[...], preferred_element_type=F32)
          + jnp.dot(h3_lo, wrh_ref[...], preferred_element_type=F32)
          + jnp.dot(h3_hi, wrl_ref[...], preferred_element_type=F32)) + br_ref[...]

    lane = lax.broadcasted_iota(jnp.int32, lg.shape, 1)
    lanef = lane.astype(F32)
    big = float(LANES)
    ninf = -jnp.inf
    gl = jnp.where(lane < N_GROUPS, lg, ninf)
    gmax = jnp.max(gl, axis=-1, keepdims=True)
    gidx = jnp.min(jnp.where(gl == gmax, lanef, big), axis=-1, keepdims=True)
    gate = 1.0 / jnp.sum(jnp.exp(gl - gmax), axis=-1, keepdims=True)
    lo = N_GROUPS + EXPERTS_PER_GROUP * gidx
    el = jnp.where((lanef >= lo) & (lanef < lo + EXPERTS_PER_GROUP), lg, ninf)
    e1 = jnp.max(el, axis=-1, keepdims=True)
    i1 = jnp.min(jnp.where(el == e1, lanef, big), axis=-1, keepdims=True)
    el2 = jnp.where(lanef == i1, ninf, el)
    e2 = jnp.max(el2, axis=-1, keepdims=True)
    i2 = jnp.min(jnp.where(el2 == e2, lanef, big), axis=-1, keepdims=True)
    tt = jnp.exp(e2 - e1)
    w1 = gate / (1.0 + tt)
    w2 = gate * tt / (1.0 + tt)

    c1 = lanef == i1
    c2 = lanef == i2
    cnt = jnp.where(c1, 1.0, 0.0) + jnp.where(c2, 1.0, 0.0)
    run = run_scr[0:1, :]
    before = jnp.dot(ltri_scr[...], cnt.astype(BF16), preferred_element_type=F32) + run
    r1 = jnp.sum(jnp.where(c1, before, 0.0), axis=-1, keepdims=True)
    r2 = jnp.sum(jnp.where(c2, before, 0.0), axis=-1, keepdims=True)
    run_scr[...] = jnp.broadcast_to(run + jnp.sum(cnt, axis=0, keepdims=True), run_scr.shape)
    counts_ref[...] = run_scr[...]

    vals = (i1 - N_GROUPS, i2 - N_GROUPS, w1, w2, r1, r2)
    slab = jnp.zeros(lg.shape, F32)
    for li, val in enumerate(vals):
        slab = jnp.where(lane == li, val, slab)
    route_ref[...] = slab


def _postmix(x2d, a_out, o_attn, w_out, norm_xq, w_q, k_mem, v_mem, w_o, norm_ffn,
             wr_hi, wr_lo, b_r, seq):
    n, d = x2d.shape
    tm = TOKEN_TILE
    per_b = seq // tm
    half = a_out.shape[1]
    row = lambda w: pl.BlockSpec((tm, w), lambda i: (i, 0))
    full = lambda a: pl.BlockSpec(a.shape, lambda i: (0,) * a.ndim)
    memspec = pl.BlockSpec((1,) + k_mem.shape[1:], lambda i: (i // per_b, 0, 0))
    return pl.pallas_call(
        _postmix_kernel,
        grid=(n // tm,),
        in_specs=[row(d), row(half), row(half), full(w_out), full(norm_xq), full(w_q),
                  memspec, memspec, full(w_o), full(norm_ffn), full(wr_hi), full(wr_lo), full(b_r)],
        out_specs=[row(d), row(d // 2), row(LANES), pl.BlockSpec((8, LANES), lambda i: (0, 0))],
        out_shape=[jax.ShapeDtypeStruct((n, d), F32), jax.ShapeDtypeStruct((n, d // 2), jnp.uint32),
                   jax.ShapeDtypeStruct((n, LANES), F32), jax.ShapeDtypeStruct((8, LANES), F32)],
        scratch_shapes=[pltpu.VMEM((tm, tm), BF16), pltpu.VMEM((8, LANES), F32)],
        compiler_params=_cparams("arbitrary"),
        name="postmix",
    )(x2d, a_out, o_attn, w_out, norm_xq, w_q, k_mem, v_mem, w_o, norm_ffn, wr_hi, wr_lo, b_r)


def _drain(copy, count):
    def body(t, carry):
        copy.wait()
        return carry
    lax.fori_loop(0, count, body, 0, unroll=8)


def _dispatch_kernel(pos_ref, h_ref, xin_ref, xout_ref, sem):
    del xin_ref
    tm = h_ref.shape[0]

    def row_copy(t, p):
        return pltpu.make_async_copy(h_ref.at[pl.ds(t, 1), :], xout_ref.at[pl.ds(p, 1), :], sem)

    def issue(t, carry):
        for k in range(TOP_K):
            row_copy(t, pos_ref[TOP_K * t + k]).start(priority=k % 2)
        return carry

    lax.fori_loop(0, tm, issue, 0, unroll=8)
    _drain(row_copy(0, 0), TOP_K * tm)


def _dispatch(pos, h3p, n_rows):
    n, w = h3p.shape
    tm = TOKEN_TILE
    x0 = jnp.zeros((n_rows, w), jnp.uint32)
    return pl.pallas_call(
        _dispatch_kernel,
        grid=(n // tm,),
        in_specs=[pl.BlockSpec((TOP_K * tm,), lambda i: (i,), memory_space=pltpu.SMEM),
                  pl.BlockSpec((tm, w), lambda i: (i, 0)),
                  pl.BlockSpec(memory_space=pl.ANY)],
        out_specs=pl.BlockSpec(memory_space=pl.ANY),
        out_shape=jax.ShapeDtypeStruct((n_rows, w), jnp.uint32),
        scratch_shapes=[pltpu.SemaphoreType.DMA(())],
        input_output_aliases={2: 0},
        compiler_params=_cparams("arbitrary"),
        name="dispatch",
    )(pos, h3p, x0)


def _expert_kernel(be_ref, nused_ref, x_ref, wg_ref, wu_ref, wd_ref, y_ref):
    i = pl.program_id(0)
    half = x_ref.shape[1]

    @pl.when(i < nused_ref[0])
    def _():
        x_lo, x_hi = _unpack_rows(x_ref[...])

        def proj(w_ref):
            return (jnp.dot(x_lo, w_ref[0, :half, :], preferred_element_type=F32)
                    + jnp.dot(x_hi, w_ref[0, half:, :], preferred_element_type=F32))

        hmid = (jax.nn.silu(proj(wg_ref)) * proj(wu_ref)).astype(BF16)
        y_ref[...] = _pack_rows(jnp.dot(hmid, wd_ref[0], preferred_element_type=F32))

    @pl.when(i >= nused_ref[0])
    def _():
        y_ref[...] = jnp.zeros(y_ref.shape, y_ref.dtype)


def _experts(block_expert, n_used, x_rows, w_gate, w_up, w_down):
    p, half = x_rows.shape
    tb = EXPERT_ROWS
    _, d, de = w_gate.shape
    grid_spec = pltpu.PrefetchScalarGridSpec(
        num_scalar_prefetch=2,
        grid=(p // tb,),
        in_specs=[pl.BlockSpec((tb, half), lambda i, be, nu: (i, 0)),
                  pl.BlockSpec((1, d, de), lambda i, be, nu: (be[i], 0, 0)),
                  pl.BlockSpec((1, d, de), lambda i, be, nu: (be[i], 0, 0)),
                  pl.BlockSpec((1, de, d), lambda i, be, nu: (be[i], 0, 0))],
        out_specs=pl.BlockSpec((tb, half), lambda i, be, nu: (i, 0)),
    )
    return pl.pallas_call(
        _expert_kernel,
        grid_spec=grid_spec,
        out_shape=jax.ShapeDtypeStruct((p, half), jnp.uint32),
        compiler_params=_cparams("arbitrary"),
        name="experts",
    )(block_expert, n_used, x_rows, w_gate, w_up, w_down)


def _combine_kernel(pos_ref, posn_ref, x2_ref, route_ref, g_ref, y_hbm, o_ref, ybuf, sems):
    i = pl.program_id(0)
    tm, d = x2_ref.shape
    half = d // 2
    slot = i % 2

    def row_copy(p, s, k, t):
        return pltpu.make_async_copy(y_hbm.at[pl.ds(p, 1), :], ybuf.at[s, k, pl.ds(t, 1), :], sems.at[s])

    def issue(p_ref, s):
        def body(t, carry):
            for k in range(TOP_K):
                row_copy(p_ref[TOP_K * t + k], s, k, t).start(priority=k % 2)
            return carry
        lax.fori_loop(0, tm, body, 0, unroll=8)

    @pl.when(i == 0)
    def _():
        issue(pos_ref, 0)

    @pl.when(i + 1 < pl.num_programs(0))
    def _():
        issue(posn_ref, 1 - slot)

    _drain(row_copy(0, slot, 0, 0), TOP_K * tm)

    r = route_ref[...]
    w1, w2 = r[:, 2:3], r[:, 3:4]
    lo0, hi0 = _unpack_rows(ybuf[slot, 0])
    lo1, hi1 = _unpack_rows(ybuf[slot, 1])
    xl = x2_ref[:, :half] + w1 * lo0.astype(F32) + w2 * lo1.astype(F32)
    xh = x2_ref[:, half:] + w1 * hi0.astype(F32) + w2 * hi1.astype(F32)
    ms = (jnp.sum(xl * xl, axis=-1, keepdims=True) + jnp.sum(xh * xh, axis=-1, keepdims=True)) / d
    inv = lax.rsqrt(ms + RMS_EPS)
    o_ref[:, :half] = xl * inv * g_ref[:, :half]
    o_ref[:, half:] = xh * inv * g_ref[:, half:]


def _combine(pos, x2, route, norm_final, y_rows):
    n, d = x2.shape
    tm = TOKEN_TILE
    steps = n // tm
    row = lambda w: pl.BlockSpec((tm, w), lambda i: (i, 0))
    pos_spec = lambda f: pl.BlockSpec((TOP_K * tm,), f, memory_space=pltpu.SMEM)
    return pl.pallas_call(
        _combine_kernel,
        grid=(steps,),
        in_specs=[pos_spec(lambda i: (i,)), pos_spec(lambda i: (jnp.minimum(i + 1, steps - 1),)),
                  row(d), row(LANES), pl.BlockSpec(norm_final.shape, lambda i: (0, 0)),
                  pl.BlockSpec(memory_space=pl.ANY)],
        out_specs=row(d),
        out_shape=jax.ShapeDtypeStruct((n, d), F32),
        scratch_shapes=[pltpu.VMEM((2, TOP_K, tm, d // 2), jnp.uint32), pltpu.SemaphoreType.DMA((2,))],
        compiler_params=_cparams("arbitrary"),
        name="combine",
    )(pos, pos, x2, route, norm_final, y_rows)


def _dispatch_plan(route, counts_slab, n_rows):
    tb = EXPERT_ROWS
    counts = counts_slab[0, N_GROUPS:N_GROUPS + N_EXPERTS].astype(jnp.int32)
    padded = (counts + tb - 1) // tb * tb
    pend = jnp.cumsum(padded)
    pstart = pend - padded
    eid = route[:, :TOP_K].astype(jnp.int32)
    rank = route[:, 4:4 + TOP_K].astype(jnp.int32)
    pos = (jnp.take(pstart, eid) + rank).reshape(-1)
    nb = n_rows // tb
    block_expert = jnp.clip(jnp.searchsorted(pend, jnp.arange(nb, dtype=jnp.int32) * tb, side='right'),
                            0, N_EXPERTS - 1).astype(jnp.int32)
    n_used = (pend[-1] // tb).astype(jnp.int32).reshape(1)
    return pos, block_expert, n_used


def kernel(x, mem, norm_mix, w_in, sgu_ln_g, sgu_ln_b, sgu_w, sgu_b, lambda_q1, lambda_k1, lambda_q2,
           lambda_k2, diff_subln, w_out, norm_xq, norm_mem, w_q_mem, w_kv_mem, w_o_mem, norm_ffn,
           w_router_group, b_router_group, w_router_expert, b_router_expert, w_gate, w_up, w_down,
           norm_final):
    bn, sn, d = x.shape
    n = bn * sn
    assert w_in.shape[0] == 1 and sn % TOKEN_TILE == 0 and sn % ATTN_TK == 0
    l = 0
    x2d = x.reshape(n, d)

    sw = SGU_GROUPS * SGU_CH
    qkw = DIFF_HEADS * 2 * DIFF_QK_DIM
    w_in_l = w_in[l].astype(BF16)
    w_uvk = jnp.concatenate([w_in_l[:, :2 * sw], w_in_l[:, 2 * sw + qkw:2 * sw + 2 * qkw]], axis=1)
    w_qv_t = jnp.concatenate([w_in_l[:, 2 * sw:2 * sw + qkw], w_in_l[:, 2 * sw + 2 * qkw:]], axis=1).T
    a_out, k, qt, vt = _inproj_sgu(x2d, norm_mix[l][None], w_uvk, w_qv_t, sgu_ln_g[l], sgu_ln_b[l],
                                   sgu_w[l], sgu_b[l].T, sn)
    lamv = jnp.stack([lambda_q1[l], lambda_k1[l], lambda_q2[l], lambda_k2[l]])
    wide = k.shape[1]
    o_attn = _diff_attn(qt, k.reshape(bn, sn, wide), vt, lamv, diff_subln[l][:, None]).reshape(n, wide)
    k_mem, v_mem = _mem_kv(mem, norm_mem[l][None], w_kv_mem[l].astype(BF16))

    w_r = jnp.concatenate([w_router_group[l], w_router_expert[l]], axis=1)
    w_r = jnp.pad(w_r, ((0, 0), (0, LANES - w_r.shape[1])))
    wr_hi = w_r.astype(BF16)
    wr_lo = (w_r - wr_hi.astype(F32)).astype(BF16)
    b_r = jnp.concatenate([b_router_group[l], b_router_expert[l]])
    b_r = jnp.pad(b_r, (0, LANES - b_r.shape[0]))[None]
    x2, h3p, route, counts = _postmix(x2d, a_out, o_attn, w_out[l].astype(BF16), norm_xq[l][None],
                                      w_q_mem[l].astype(BF16), k_mem, v_mem, w_o_mem[l].astype(BF16),
                                      norm_ffn[l][None], wr_hi, wr_lo, b_r, sn)

    n_rows = n * TOP_K + N_EXPERTS * EXPERT_ROWS
    pos, block_expert, n_used = _dispatch_plan(route, counts, n_rows)
    x_rows = _dispatch(pos, h3p, n_rows)
    y_rows = _experts(block_expert, n_used, x_rows, w_gate[l].astype(BF16), w_up[l].astype(BF16),
                      w_down[l].astype(BF16))
    out = _combine(pos, x2, route, norm_final[None], y_rows)
    return out.reshape(bn, sn, d)
```

```python
import math

import jax
import jax.numpy as jnp
from jax import lax
from jax.experimental import pallas as pl
from jax.experimental.pallas import tpu as pltpu

F32 = jnp.float32
BF16 = jnp.bfloat16

LANES = 128
SGU_GROUPS = 4
SGU_CH = 128
CHUNK = 128
DIFF_HEADS = 4
DIFF_QK_DIM = 64
DIFF_V_DIM = 128
MEM_HEADS = 4
N_GROUPS = 4
EXPERTS_PER_GROUP = 8
N_EXPERTS = N_GROUPS * EXPERTS_PER_GROUP
PAIRS_PER_GROUP = EXPERTS_PER_GROUP * (EXPERTS_PER_GROUP - 1) // 2
N_BUCKETS = N_GROUPS * PAIRS_PER_GROUP
TOP_K = 2
RMS_EPS = 1e-6
LN_EPS = 1e-5
LOG2E = math.log2(math.e)
LAMBDA_INIT = 0.8 - 0.6 * math.exp(-0.3 * 0)

TOKEN_TILE = 512
ATTN_TQ = 512
ATTN_TK = 512
ONES_ROWS = 16
MOE_TILE = 1024
EXPERT_ROWS = 256
SUBLANES = 8
VMEM_LIMIT = 56 * 1024 * 1024


def _cparams(*sem):
    return pltpu.CompilerParams(dimension_semantics=sem, vmem_limit_bytes=VMEM_LIMIT)


def _gelu(x):
    return 0.5 * x * (1.0 + lax.erf(x * (2.0 ** -0.5)))


def _rms(x, g):
    ms = jnp.mean(x * x, axis=-1, keepdims=True)
    return x * lax.rsqrt(ms + RMS_EPS) * g


def _inproj_sgu_kernel(x_ref, g_ref, w_ref, wt_ref, lng_ref, lnb_ref, ws_ref, bs_ref,
                       a_ref, k_ref, qt_ref, vt_ref):
    tm = x_ref.shape[0]
    h = _rms(x_ref[...], g_ref[...]).astype(BF16)
    sw = SGU_GROUPS * SGU_CH
    qkw = DIFF_HEADS * 2 * DIFF_QK_DIM

    def proj(c0, c1):
        return jnp.dot(h, w_ref[:, c0:c1], preferred_element_type=F32)

    def proj_t(r0, r1):
        return lax.dot_general(wt_ref[r0:r1, :], h, (((1,), (1,)), ((), ())), preferred_element_type=F32)

    u = _gelu(proj(0, sw))
    v = _gelu(proj(sw, 2 * sw))
    k_ref[...] = proj(2 * sw, 2 * sw + qkw).astype(BF16)
    qt_ref[0] = (proj_t(0, qkw) * (DIFF_QK_DIM ** -0.5 * LOG2E)).astype(BF16)
    vt = proj_t(qkw, qkw + DIFF_HEADS * DIFF_V_DIM)
    ones = jnp.ones((ONES_ROWS, tm), F32)
    pieces = []
    for hh in range(DIFF_HEADS):
        pieces += [vt[hh * DIFF_V_DIM:(hh + 1) * DIFF_V_DIM, :], ones]
    vt_ref[0] = jnp.concatenate(pieces, axis=0).astype(BF16)

    row = lax.broadcasted_iota(jnp.int32, (CHUNK, CHUNK), 0)
    col = lax.broadcasted_iota(jnp.int32, (CHUNK, CHUNK), 1)
    nchunk = tm // CHUNK
    for g in range(SGU_GROUPS):
        cs = slice(g * SGU_CH, (g + 1) * SGU_CH)
        vg = v[:, cs]
        mu = jnp.mean(vg, axis=-1, keepdims=True)
        d = vg - mu
        var = jnp.mean(d * d, axis=-1, keepdims=True)
        vn = (d * lax.rsqrt(var + LN_EPS) * lng_ref[g:g + 1, :] + lnb_ref[g:g + 1, :]).astype(BF16)
        rhs = jnp.concatenate([vn[c * CHUNK:(c + 1) * CHUNK, :] for c in range(nchunk)], axis=1)
        wt = jnp.where(row >= col, ws_ref[g], 0.0).astype(BF16)
        sp = jnp.dot(wt, rhs, preferred_element_type=F32) + bs_ref[:, g:g + 1]
        for c in range(nchunk):
            rs = slice(c * CHUNK, (c + 1) * CHUNK)
            a_ref[rs, cs] = (u[rs, cs] * sp[:, c * SGU_CH:(c + 1) * SGU_CH]).astype(BF16)


def _inproj_sgu(x2d, norm_mix, w_uvk, w_qv_t, ln_g, ln_b, w_s, b_s_t, seq):
    n, d = x2d.shape
    tm = TOKEN_TILE
    per_b = seq // tm
    wide = DIFF_HEADS * DIFF_V_DIM
    row_spec = pl.BlockSpec((tm, wide), lambda i: (i, 0))
    vt_rows = DIFF_HEADS * (DIFF_V_DIM + ONES_ROWS)
    t_spec = lambda rows: pl.BlockSpec((1, rows, tm), lambda i: (i // per_b, 0, i % per_b))
    full = lambda a: pl.BlockSpec(a.shape, lambda i: (0,) * a.ndim)
    row_sds = jax.ShapeDtypeStruct((n, wide), BF16)
    t_sds = lambda rows: jax.ShapeDtypeStruct((n // seq, rows, seq), BF16)
    return pl.pallas_call(
        _inproj_sgu_kernel,
        grid=(n // tm,),
        in_specs=[pl.BlockSpec((tm, d), lambda i: (i, 0)), full(norm_mix), full(w_uvk), full(w_qv_t),
                  full(ln_g), full(ln_b), full(w_s), full(b_s_t)],
        out_specs=[row_spec, row_spec, t_spec(wide), t_spec(vt_rows)],
        out_shape=[row_sds, row_sds, t_sds(wide), t_sds(vt_rows)],
        compiler_params=_cparams("parallel"),
        name="inproj_sgu",
    )(x2d, norm_mix, w_uvk, w_qv_t, ln_g, ln_b, w_s, b_s_t)


def _diff_attn_kernel(qt_ref, k_ref, vt_ref, lamv_ref, sub_ref, o_ref, kb_scr, acc_scr, s0_scr, s1_scr):
    dqk = DIFF_QK_DIM
    tq = qt_ref.shape[2]
    tk = kb_scr.shape[1]
    h = pl.program_id(1)
    iq = pl.program_id(2)
    slope = jnp.exp2(-(8.0 / DIFF_HEADS) * jnp.full((1, 1), h + 1, jnp.int32).astype(F32))

    qt = qt_ref[0]
    frow = lax.broadcasted_iota(jnp.int32, qt.shape, 0)
    zero = jnp.zeros_like(qt)
    qw = jnp.concatenate([jnp.where(frow < dqk, qt, zero), jnp.where(frow >= dqk, qt, zero)], axis=1)

    slope2 = slope * LOG2E
    nvar = tk // tq

    @pl.when(iq == 0)
    def _():
        krow = lax.broadcasted_iota(jnp.int32, (tk, 2 * tq), 0)
        qcol = lax.broadcasted_iota(jnp.int32, (tk, 2 * tq), 1) % tq
        kb = slope2 * krow.astype(F32)
        kb_scr[0] = kb
        for v in range(nvar):
            kb_scr[1 + v] = jnp.where(krow - v * tq <= qcol, kb, -jnp.inf)
        kb_scr[1 + nvar] = jnp.full((tk, 2 * tq), -jnp.inf, F32)

    acc_scr[...] = jnp.zeros(acc_scr.shape, F32)
    q0 = iq * tq
    n_full = q0 // tk
    last_bias = 1 + iq % nvar
    n_pairs = (n_full + 2) // 2
    max_blk = k_ref.shape[1] // tk - 1

    def block_offset(j):
        off = pl.multiple_of(jnp.minimum(j, max_blk) * tk, tk)
        return off, slope2 * (off - q0).astype(F32)

    def scores(j, buf):
        off, cj = block_offset(j)
        sel = jnp.where(j < n_full, 0, jnp.where(j == n_full, last_bias, 1 + nvar))
        s = jnp.dot(k_ref[0, pl.ds(off, tk), :], qw, preferred_element_type=F32) + kb_scr[sel]
        buf[...] = s
        return jnp.max(s, axis=0, keepdims=True) + cj

    def accumulate(j, buf, bm, m_prev):
        off, cj = block_offset(j)
        m_new = jnp.maximum(m_prev, bm)
        p = jnp.exp2(buf[...] - (m_new - cj))
        alpha = jnp.exp2(m_prev - m_new)
        pv = jnp.dot(vt_ref[0, :, pl.ds(off, tk)], p.astype(BF16), preferred_element_type=F32)
        acc_scr[...] = alpha * acc_scr[...] + pv
        return m_new

    def body(t, c):
        m, bm0 = c
        bm1 = scores(2 * t + 1, s1_scr)
        m = accumulate(2 * t, s0_scr, bm0, m)
        bm0 = scores(2 * t + 2, s0_scr)
        m = accumulate(2 * t + 1, s1_scr, bm1, m)
        return m, bm0

    m0 = jnp.full((1, 2 * tq), -jnp.inf, F32)
    bm0 = scores(0, s0_scr)
    m, bm0 = lax.fori_loop(0, n_pairs - 1, body, (m0, bm0))
    last = 2 * n_pairs - 1
    bm1 = scores(last, s1_scr)
    m = accumulate(last - 1, s0_scr, bm0, m)
    m = accumulate(last, s1_scr, bm1, m)

    lam = (jnp.exp(jnp.sum(lamv_ref[0:1, :] * lamv_ref[1:2, :], axis=-1, keepdims=True))
           - jnp.exp(jnp.sum(lamv_ref[2:3, :] * lamv_ref[3:4, :], axis=-1, keepdims=True))
           + LAMBDA_INIT)
    dv = DIFF_V_DIM
    on = acc_scr[:dv, :] * (1.0 / acc_scr[dv:dv + 1, :])
    o = on[:, :tq] - lam * on[:, tq:]
    ms = jnp.mean(o * o, axis=0, keepdims=True)
    o = o * lax.rsqrt(ms + RMS_EPS) * sub_ref[...] * (1.0 - LAMBDA_INIT)
    o_ref[0] = o.T.astype(o_ref.dtype)


def _diff_attn(qt, k, vt, lamv, subln_col):
    b, s, _ = k.shape
    tq, tk = ATTN_TQ, ATTN_TK
    dv = DIFF_V_DIM
    assert tk % tq == 0 and s % tk == 0
    return pl.pallas_call(
        _diff_attn_kernel,
        grid=(b, DIFF_HEADS, s // tq),
        in_specs=[pl.BlockSpec((1, 2 * DIFF_QK_DIM, tq), lambda bi, hi, qi: (bi, hi, qi)),
                  pl.BlockSpec((1, s, 2 * DIFF_QK_DIM), lambda bi, hi, qi: (bi, 0, hi)),
                  pl.BlockSpec((1, dv + ONES_ROWS, s), lambda bi, hi, qi: (bi, hi, 0)),
                  pl.BlockSpec(lamv.shape, lambda bi, hi, qi: (0, 0)),
                  pl.BlockSpec(subln_col.shape, lambda bi, hi, qi: (0, 0))],
        out_specs=pl.BlockSpec((1, tq, dv), lambda bi, hi, qi: (bi, qi, hi)),
        out_shape=jax.ShapeDtypeStruct((b, s, DIFF_HEADS * dv), BF16),
        scratch_shapes=[pltpu.VMEM((2 + tk // tq, tk, 2 * tq), F32),
                        pltpu.VMEM((dv + ONES_ROWS, 2 * tq), F32),
                        pltpu.VMEM((tk, 2 * tq), F32), pltpu.VMEM((tk, 2 * tq), F32)],
        compiler_params=_cparams("arbitrary", "arbitrary", "arbitrary"),
        name="diff_attn",
    )(qt, k, vt, lamv, subln_col)


def _mem_kv_kernel(mem_ref, g_ref, w_ref, k_ref, v_ref):
    d = mem_ref.shape[-1]
    mn = _rms(mem_ref[0], g_ref[...]).astype(BF16)
    k_ref[0] = jnp.dot(mn, w_ref[:, :d], preferred_element_type=F32).astype(BF16)
    v_ref[0] = jnp.dot(mn, w_ref[:, d:], preferred_element_type=F32).astype(BF16)


def _mem_kv(mem, norm_mem, w_kv):
    b, m, d = mem.shape
    spec = pl.BlockSpec((1, m, d), lambda i: (i, 0, 0))
    sds = jax.ShapeDtypeStruct((b, m, d), BF16)
    return pl.pallas_call(
        _mem_kv_kernel,
        grid=(b,),
        in_specs=[spec, pl.BlockSpec(norm_mem.shape, lambda i: (0, 0)),
                  pl.BlockSpec(w_kv.shape, lambda i: (0, 0))],
        out_specs=[spec, spec],
        out_shape=[sds, sds],
        compiler_params=_cparams("parallel"),
        name="mem_kv",
    )(mem, norm_mem, w_kv)


def _pack_rows(x):
    w = x.shape[1] // 2
    bits = pltpu.bitcast(x.astype(BF16).astype(F32), jnp.uint32)
    return (bits[:, :w] >> 16) | bits[:, w:]


def _unpack_rows(u):
    lo = pltpu.bitcast(u << 16, F32).astype(BF16)
    hi = pltpu.bitcast(u & jnp.uint32(0xFFFF0000), F32).astype(BF16)
    return lo, hi


def _postmix_kernel(x_ref, a_ref, o_ref, wout_ref, gq_ref, wq_ref, km_ref, vm_ref, wo_ref,
                    gf_ref, wrh_ref, wrl_ref, br_ref, x2_ref, h3_ref, route_ref, counts_ref,
                    ltri_scr, run_scr):
    tm, d = x_ref.shape

    @pl.when(pl.program_id(0) == 0)
    def _():
        r = lax.broadcasted_iota(jnp.int32, (tm, tm), 0)
        c = lax.broadcasted_iota(jnp.int32, (tm, tm), 1)
        ltri_scr[...] = jnp.where(c < r, 1.0, 0.0).astype(BF16)
        run_scr[...] = jnp.zeros(run_scr.shape, F32)

    half = a_ref.shape[1]
    hd = d // MEM_HEADS
    mixed = (jnp.dot(a_ref[...], wout_ref[:half, :], preferred_element_type=F32)
             + jnp.dot(o_ref[...], wout_ref[half:, :], preferred_element_type=F32))
    x1 = x_ref[...] + mixed

    hq = _rms(x1, gq_ref[...]).astype(BF16)
    q = (jnp.dot(hq, wq_ref[...], preferred_element_type=F32) * (hd ** -0.5)).astype(BF16)
    heads = []
    for hh in range(MEM_HEADS):
        cs = slice(hh * hd, (hh + 1) * hd)
        s = lax.dot_general(q[:, cs], km_ref[0, :, cs], (((1,), (1,)), ((), ())),
                            preferred_element_type=F32)
        e = jnp.exp(s - jnp.max(s, axis=-1, keepdims=True))
        p = e / jnp.sum(e, axis=-1, keepdims=True)
        heads.append(jnp.dot(p.astype(BF16), vm_ref[0, :, cs], preferred_element_type=F32).astype(BF16))
    x2 = x1 + jnp.dot(jnp.concatenate(heads, axis=1), wo_ref[...], preferred_element_type=F32)
    x2_ref[...] = x2

    h3 = _rms(x2, gf_ref[...])
    h3_hi = h3.astype(BF16)
    h3_lo = (h3 - h3_hi.astype(F32)).astype(BF16)
    lg = (jnp.dot(h3_hi, wrh_ref[...], preferred_element_type=F32)
          + jnp.dot(h3_lo, wrh_ref[...], preferred_element_type=F32)
          + jnp.dot(h3_hi, wrl_ref[...], preferred_element_type=F32)) + br_ref[...]

    lane = lax.broadcasted_iota(jnp.int32, lg.shape, 1)
    lanef = lane.astype(F32)
    big = float(LANES)
    ninf = -jnp.inf
    gl = jnp.where(lane < N_GROUPS, lg, ninf)
    gmax = jnp.max(gl, axis=-1, keepdims=True)
    gidx = jnp.min(jnp.where(gl == gmax, lanef, big), axis=-1, keepdims=True)
    gate = 1.0 / jnp.sum(jnp.exp(gl - gmax), axis=-1, keepdims=True)
    lo = N_GROUPS + EXPERTS_PER_GROUP * gidx
    el = jnp.where((lanef >= lo) & (lanef < lo + EXPERTS_PER_GROUP), lg, ninf)
    e1 = jnp.max(el, axis=-1, keepdims=True)
    i1 = jnp.min(jnp.where(el == e1, lanef, big), axis=-1, keepdims=True)
    el2 = jnp.where(lanef == i1, ninf, el)
    e2 = jnp.max(el2, axis=-1, keepdims=True)
    i2 = jnp.min(jnp.where(el2 == e2, lanef, big), axis=-1, keepdims=True)
    tt = jnp.exp(e2 - e1)
    w1 = gate / (1.0 + tt)
    w2 = gate * tt / (1.0 + tt)

    e_lo = jnp.minimum(i1, i2) - lo
    e_hi = jnp.maximum(i1, i2) - lo
    pair = e_lo * (2 * EXPERTS_PER_GROUP - 1 - e_lo) * 0.5 + (e_hi - e_lo - 1.0)
    bucket = gidx * PAIRS_PER_GROUP + pair
    first_is_lo = i1 < i2
    w_lo = jnp.where(first_is_lo, w1, w2)
    w_hi = jnp.where(first_is_lo, w2, w1)

    hit = lanef == bucket
    cnt = jnp.where(hit, 1.0, 0.0)
    run = run_scr[0:1, :]
    before = jnp.dot(ltri_scr[...], cnt.astype(BF16), preferred_element_type=F32) + run
    rank = jnp.sum(jnp.where(hit, before, 0.0), axis=-1, keepdims=True)
    run_scr[...] = jnp.broadcast_to(run + jnp.sum(cnt, axis=0, keepdims=True), run_scr.shape)
    counts_ref[...] = run_scr[...]

    slab = jnp.zeros(lg.shape, F32)
    for li, val in enumerate((bucket, rank)):
        slab = jnp.where(lane == li, val, slab)
    route_ref[...] = slab

    gates = jnp.where(lane == 0, w_lo, jnp.where(lane == 1, w_hi, 0.0))
    h3_ref[...] = jnp.concatenate([_pack_rows(h3), pltpu.bitcast(gates, jnp.uint32)], axis=1)


def _postmix(x2d, a_out, o_attn, w_out, norm_xq, w_q, k_mem, v_mem, w_o, norm_ffn,
             wr_hi, wr_lo, b_r, seq):
    n, d = x2d.shape
    tm = TOKEN_TILE
    per_b = seq // tm
    half = a_out.shape[1]
    row = lambda w: pl.BlockSpec((tm, w), lambda i: (i, 0))
    full = lambda a: pl.BlockSpec(a.shape, lambda i: (0,) * a.ndim)
    memspec = pl.BlockSpec((1,) + k_mem.shape[1:], lambda i: (i // per_b, 0, 0))
    return pl.pallas_call(
        _postmix_kernel,
        grid=(n // tm,),
        in_specs=[row(d), row(half), row(half), full(w_out), full(norm_xq), full(w_q),
                  memspec, memspec, full(w_o), full(norm_ffn), full(wr_hi), full(wr_lo), full(b_r)],
        out_specs=[row(d), row(d // 2 + LANES), row(LANES), pl.BlockSpec((8, LANES), lambda i: (0, 0))],
        out_shape=[jax.ShapeDtypeStruct((n, d), F32), jax.ShapeDtypeStruct((n, d // 2 + LANES), jnp.uint32),
                   jax.ShapeDtypeStruct((n, LANES), F32), jax.ShapeDtypeStruct((8, LANES), F32)],
        scratch_shapes=[pltpu.VMEM((tm, tm), BF16), pltpu.VMEM((8, LANES), F32)],
        compiler_params=_cparams("arbitrary"),
        name="postmix",
    )(x2d, a_out, o_attn, w_out, norm_xq, w_q, k_mem, v_mem, w_o, norm_ffn, wr_hi, wr_lo, b_r)


def _drain(copy, count):
    def body(t, carry):
        copy.wait()
        return carry
    lax.fori_loop(0, count, body, 0, unroll=8)


def _dispatch_kernel(pos_ref, h_ref, xin_ref, xout_ref, sem):
    del xin_ref
    groups = h_ref.shape[0]

    def row_copy(g, j, p):
        return pltpu.make_async_copy(h_ref.at[g, pl.ds(j, 1), :], xout_ref.at[pl.ds(p, 1), :], sem)

    def issue(g, carry):
        for j in range(SUBLANES):
            row_copy(g, j, pos_ref[SUBLANES * g + j]).start(priority=j % 2)
        return carry

    lax.fori_loop(0, groups, issue, 0)
    _drain(row_copy(0, 0, 0), groups * SUBLANES)


def _dispatch(pos, h3p, n_rows):
    n, w = h3p.shape
    tm = MOE_TILE
    x0 = jnp.zeros((n_rows, w), jnp.uint32)
    return pl.pallas_call(
        _dispatch_kernel,
        grid=(n // tm,),
        in_specs=[pl.BlockSpec((tm,), lambda i: (i,), memory_space=pltpu.SMEM),
                  pl.BlockSpec((tm // SUBLANES, SUBLANES, w), lambda i: (i, 0, 0)),
                  pl.BlockSpec(memory_space=pl.ANY)],
        out_specs=pl.BlockSpec(memory_space=pl.ANY),
        out_shape=jax.ShapeDtypeStruct((n_rows, w), jnp.uint32),
        scratch_shapes=[pltpu.SemaphoreType.DMA(())],
        input_output_aliases={2: 0},
        compiler_params=_cparams("arbitrary"),
        name="dispatch",
    )(pos, h3p.reshape(n // SUBLANES, SUBLANES, w), x0)


def _expert_kernel(ea_ref, eb_ref, nused_ref, x_ref, wga_ref, wua_ref, wda_ref, wgb_ref, wub_ref, wdb_ref,
                   y_ref):
    del ea_ref, eb_ref
    i = pl.program_id(0)
    half = y_ref.shape[1]

    @pl.when(i < nused_ref[0])
    def _():
        x_lo, x_hi = _unpack_rows(x_ref[:, :half])
        gates = pltpu.bitcast(x_ref[:, half:], F32)

        def expert(wg_ref, wu_ref, wd_ref):
            def proj(w_ref):
                return (jnp.dot(x_lo, w_ref[0, :half, :], preferred_element_type=F32)
                        + jnp.dot(x_hi, w_ref[0, half:, :], preferred_element_type=F32))
            hmid = (jax.nn.silu(proj(wg_ref)) * proj(wu_ref)).astype(BF16)
            return jnp.dot(hmid, wd_ref[0], preferred_element_type=F32)

        y = (gates[:, 0:1] * expert(wga_ref, wua_ref, wda_ref)
             + gates[:, 1:2] * expert(wgb_ref, wub_ref, wdb_ref))
        y_ref[...] = _pack_rows(y)

    @pl.when(i >= nused_ref[0])
    def _():
        y_ref[...] = jnp.zeros(y_ref.shape, y_ref.dtype)


def _experts(block_ea, block_eb, n_used, x_rows, w_gate, w_up, w_down):
    p, w = x_rows.shape
    tb = EXPERT_ROWS
    _, d, de = w_gate.shape
    up = lambda sel: pl.BlockSpec((1, d, de), lambda i, ea, eb, nu: (sel(ea, eb)[i], 0, 0))
    down = lambda sel: pl.BlockSpec((1, de, d), lambda i, ea, eb, nu: (sel(ea, eb)[i], 0, 0))
    first = lambda ea, eb: ea
    second = lambda ea, eb: eb
    grid_spec = pltpu.PrefetchScalarGridSpec(
        num_scalar_prefetch=3,
        grid=(p // tb,),
        in_specs=[pl.BlockSpec((tb, w), lambda i, ea, eb, nu: (i, 0)),
                  up(first), up(first), down(first), up(second), up(second), down(second)],
        out_specs=pl.BlockSpec((tb, d // 2), lambda i, ea, eb, nu: (i, 0)),
    )
    return pl.pallas_call(
        _expert_kernel,
        grid_spec=grid_spec,
        out_shape=jax.ShapeDtypeStruct((p, d // 2), jnp.uint32),
        compiler_params=_cparams("arbitrary"),
        name="experts",
    )(block_ea, block_eb, n_used, x_rows, w_gate, w_up, w_down, w_gate, w_up, w_down)


def _combine_kernel(pos_ref, posn_ref, x2_ref, g_ref, y_hbm, o_ref, ybuf, sems):
    i = pl.program_id(0)
    tm, d = x2_ref.shape
    half = d // 2
    groups = tm // SUBLANES
    slot = i % 2

    def row_copy(p, s, g, j):
        return pltpu.make_async_copy(y_hbm.at[pl.ds(p, 1), :], ybuf.at[s, g, pl.ds(j, 1), :], sems.at[s])

    def issue(p_ref, s):
        def body(g, carry):
            for j in range(SUBLANES):
                row_copy(p_ref[SUBLANES * g + j], s, g, j).start(priority=j % 2)
            return carry
        lax.fori_loop(0, groups, body, 0)

    @pl.when(i == 0)
    def _():
        issue(pos_ref, 0)

    @pl.when(i + 1 < pl.num_programs(0))
    def _():
        issue(posn_ref, 1 - slot)

    _drain(row_copy(0, slot, 0, 0), tm)

    y_lo, y_hi = _unpack_rows(ybuf[slot].reshape(tm, half))
    xl = x2_ref[:, :half] + y_lo.astype(F32)
    xh = x2_ref[:, half:] + y_hi.astype(F32)
    ms = (jnp.sum(xl * xl, axis=-1, keepdims=True) + jnp.sum(xh * xh, axis=-1, keepdims=True)) / d
    inv = lax.rsqrt(ms + RMS_EPS)
    o_ref[:, :half] = xl * inv * g_ref[:, :half]
    o_ref[:, half:] = xh * inv * g_ref[:, half:]


def _combine(pos, x2, norm_final, y_rows):
    n, d = x2.shape
    tm = MOE_TILE
    steps = n // tm
    row = lambda w: pl.BlockSpec((tm, w), lambda i: (i, 0))
    pos_spec = lambda f: pl.BlockSpec((tm,), f, memory_space=pltpu.SMEM)
    return pl.pallas_call(
        _combine_kernel,
        grid=(steps,),
        in_specs=[pos_spec(lambda i: (i,)), pos_spec(lambda i: (jnp.minimum(i + 1, steps - 1),)),
                  row(d), pl.BlockSpec(norm_final.shape, lambda i: (0, 0)),
                  pl.BlockSpec(memory_space=pl.ANY)],
        out_specs=row(d),
        out_shape=jax.ShapeDtypeStruct((n, d), F32),
        scratch_shapes=[pltpu.VMEM((2, tm // SUBLANES, SUBLANES, d // 2), jnp.uint32),
                        pltpu.SemaphoreType.DMA((2,))],
        compiler_params=_cparams("arbitrary"),
        name="combine",
    )(pos, pos, x2, norm_final, y_rows)


def _bucket_experts():
    ea, eb = [], []
    for g in range(N_GROUPS):
        for a in range(EXPERTS_PER_GROUP):
            for b in range(a + 1, EXPERTS_PER_GROUP):
                ea.append(g * EXPERTS_PER_GROUP + a)
                eb.append(g * EXPERTS_PER_GROUP + b)
    return jnp.asarray(ea, jnp.int32), jnp.asarray(eb, jnp.int32)


def _dispatch_plan(route, counts_slab, n_rows):
    tb = EXPERT_ROWS
    counts = counts_slab[0, :N_BUCKETS].astype(jnp.int32)
    padded = (counts + tb - 1) // tb * tb
    pend = jnp.cumsum(padded)
    pstart = pend - padded
    bucket = route[:, 0].astype(jnp.int32)
    rank = route[:, 1].astype(jnp.int32)
    pos = jnp.take(pstart, bucket) + rank
    nb = n_rows // tb
    block_start = jnp.arange(nb, dtype=jnp.int32) * tb
    block_bucket = jnp.minimum(jnp.sum((pend[None, :] <= block_start[:, None]).astype(jnp.int32), axis=1),
                               N_BUCKETS - 1)
    ea, eb = _bucket_experts()
    n_used = (pend[-1] // tb).astype(jnp.int32).reshape(1)
    return pos, jnp.take(ea, block_bucket), jnp.take(eb, block_bucket), n_used


def kernel(x, mem, norm_mix, w_in, sgu_ln_g, sgu_ln_b, sgu_w, sgu_b, lambda_q1, lambda_k1, lambda_q2,
           lambda_k2, diff_subln, w_out, norm_xq, norm_mem, w_q_mem, w_kv_mem, w_o_mem, norm_ffn,
           w_router_group, b_router_group, w_router_expert, b_router_expert, w_gate, w_up, w_down,
           norm_final):
    bn, sn, d = x.shape
    n = bn * sn
    assert w_in.shape[0] == 1 and sn % TOKEN_TILE == 0 and sn % ATTN_TK == 0 and n % MOE_TILE == 0
    l = 0
    x2d = x.reshape(n, d)

    sw = SGU_GROUPS * SGU_CH
    qkw = DIFF_HEADS * 2 * DIFF_QK_DIM
    w_in_l = w_in[l].astype(BF16)
    w_uvk = jnp.concatenate([w_in_l[:, :2 * sw], w_in_l[:, 2 * sw + qkw:2 * sw + 2 * qkw]], axis=1)
    w_qv_t = jnp.concatenate([w_in_l[:, 2 * sw:2 * sw + qkw], w_in_l[:, 2 * sw + 2 * qkw:]], axis=1).T
    a_out, k, qt, vt = _inproj_sgu(x2d, norm_mix[l][None], w_uvk, w_qv_t, sgu_ln_g[l], sgu_ln_b[l],
                                   sgu_w[l], sgu_b[l].T, sn)
    lamv = jnp.stack([lambda_q1[l], lambda_k1[l], lambda_q2[l], lambda_k2[l]])
    wide = k.shape[1]
    o_attn = _diff_attn(qt, k.reshape(bn, sn, wide), vt, lamv, diff_subln[l][:, None]).reshape(n, wide)
    k_mem, v_mem = _mem_kv(mem, norm_mem[l][None], w_kv_mem[l].astype(BF16))

    w_r = jnp.concatenate([w_router_group[l], w_router_expert[l]], axis=1)
    w_r = jnp.pad(w_r, ((0, 0), (0, LANES - w_r.shape[1])))
    wr_hi = w_r.astype(BF16)
    wr_lo = (w_r - wr_hi.astype(F32)).astype(BF16)
    b_r = jnp.concatenate([b_router_group[l], b_router_expert[l]])
    b_r = jnp.pad(b_r, (0, LANES - b_r.shape[0]))[None]
    x2, h3p, route, counts = _postmix(x2d, a_out, o_attn, w_out[l].astype(BF16), norm_xq[l][None],
                                      w_q_mem[l].astype(BF16), k_mem, v_mem, w_o_mem[l].astype(BF16),
                                      norm_ffn[l][None], wr_hi, wr_lo, b_r, sn)

    n_rows = n + N_BUCKETS * EXPERT_ROWS
    pos, block_ea, block_eb, n_used = _dispatch_plan(route, counts, n_rows)
    x_rows = _dispatch(pos, h3p, n_rows)
    y_rows = _experts(block_ea, block_eb, n_used, x_rows, w_gate[l].astype(BF16), w_up[l].astype(BF16),
                      w_down[l].astype(BF16))
    out = _combine(pos, x2, norm_final[None], y_rows)
    return out.reshape(bn, sn, d)
```

```python
import math

import jax
import jax.numpy as jnp
from jax import lax
from jax.experimental import pallas as pl
from jax.experimental.pallas import tpu as pltpu

F32 = jnp.float32
BF16 = jnp.bfloat16

LANES = 128
SGU_GROUPS = 4
SGU_CH = 128
CHUNK = 128
DIFF_HEADS = 4
DIFF_QK_DIM = 64
DIFF_V_DIM = 128
MEM_HEADS = 4
N_GROUPS = 4
EXPERTS_PER_GROUP = 8
N_EXPERTS = N_GROUPS * EXPERTS_PER_GROUP
PAIRS_PER_GROUP = EXPERTS_PER_GROUP * (EXPERTS_PER_GROUP - 1) // 2
N_BUCKETS = N_GROUPS * PAIRS_PER_GROUP
TOP_K = 2
RMS_EPS = 1e-6
LN_EPS = 1e-5
LOG2E = math.log2(math.e)
LAMBDA_INIT = 0.8 - 0.6 * math.exp(-0.3 * 0)

TOKEN_TILE = 512
ATTN_TQ = 512
ATTN_TK = 512
ONES_ROWS = 16
MOE_TILE = 1024
EXPERT_ROWS = 256
SUBLANES = 8
VMEM_LIMIT = 56 * 1024 * 1024


def _cparams(*sem):
    return pltpu.CompilerParams(dimension_semantics=sem, vmem_limit_bytes=VMEM_LIMIT)


def _gelu(x):
    return 0.5 * x * (1.0 + lax.erf(x * (2.0 ** -0.5)))


def _rms(x, g):
    ms = jnp.mean(x * x, axis=-1, keepdims=True)
    return x * lax.rsqrt(ms + RMS_EPS) * g


def _inproj_sgu_kernel(x_ref, g_ref, w_ref, wt_ref, lng_ref, lnb_ref, ws_ref, bs_ref,
                       a_ref, k_ref, qt_ref, vt_ref):
    tm = x_ref.shape[0]
    h = _rms(x_ref[...], g_ref[...]).astype(BF16)
    sw = SGU_GROUPS * SGU_CH
    qkw = DIFF_HEADS * 2 * DIFF_QK_DIM

    def proj(c0, c1):
        return jnp.dot(h, w_ref[:, c0:c1], preferred_element_type=F32)

    def proj_t(r0, r1):
        return lax.dot_general(wt_ref[r0:r1, :], h, (((1,), (1,)), ((), ())), preferred_element_type=F32)

    u = _gelu(proj(0, sw))
    v = _gelu(proj(sw, 2 * sw))
    k_ref[...] = proj(2 * sw, 2 * sw + qkw).astype(BF16)
    qt_ref[0] = (proj_t(0, qkw) * (DIFF_QK_DIM ** -0.5 * LOG2E)).astype(BF16)
    vt = proj_t(qkw, qkw + DIFF_HEADS * DIFF_V_DIM)
    ones = jnp.ones((ONES_ROWS, tm), F32)
    pieces = []
    for hh in range(DIFF_HEADS):
        pieces += [vt[hh * DIFF_V_DIM:(hh + 1) * DIFF_V_DIM, :], ones]
    vt_ref[0] = jnp.concatenate(pieces, axis=0).astype(BF16)

    row = lax.broadcasted_iota(jnp.int32, (CHUNK, CHUNK), 0)
    col = lax.broadcasted_iota(jnp.int32, (CHUNK, CHUNK), 1)
    nchunk = tm // CHUNK
    for g in range(SGU_GROUPS):
        cs = slice(g * SGU_CH, (g + 1) * SGU_CH)
        vg = v[:, cs]
        mu = jnp.mean(vg, axis=-1, keepdims=True)
        d = vg - mu
        var = jnp.mean(d * d, axis=-1, keepdims=True)
        vn = (d * lax.rsqrt(var + LN_EPS) * lng_ref[g:g + 1, :] + lnb_ref[g:g + 1, :]).astype(BF16)
        rhs = jnp.concatenate([vn[c * CHUNK:(c + 1) * CHUNK, :] for c in range(nchunk)], axis=1)
        wt = jnp.where(row >= col, ws_ref[g], 0.0).astype(BF16)
        sp = jnp.dot(wt, rhs, preferred_element_type=F32) + bs_ref[:, g:g + 1]
        for c in range(nchunk):
            rs = slice(c * CHUNK, (c + 1) * CHUNK)
            a_ref[rs, cs] = (u[rs, cs] * sp[:, c * SGU_CH:(c + 1) * SGU_CH]).astype(BF16)


def _inproj_sgu(x2d, norm_mix, w_uvk, w_qv_t, ln_g, ln_b, w_s, b_s_t, seq):
    n, d = x2d.shape
    tm = TOKEN_TILE
    per_b = seq // tm
    wide = DIFF_HEADS * DIFF_V_DIM
    row_spec = pl.BlockSpec((tm, wide), lambda i: (i, 0))
    vt_rows = DIFF_HEADS * (DIFF_V_DIM + ONES_ROWS)
    t_spec = lambda rows: pl.BlockSpec((1, rows, tm), lambda i: (i // per_b, 0, i % per_b))
    full = lambda a: pl.BlockSpec(a.shape, lambda i: (0,) * a.ndim)
    row_sds = jax.ShapeDtypeStruct((n, wide), BF16)
    t_sds = lambda rows: jax.ShapeDtypeStruct((n // seq, rows, seq), BF16)
    return pl.pallas_call(
        _inproj_sgu_kernel,
        grid=(n // tm,),
        in_specs=[pl.BlockSpec((tm, d), lambda i: (i, 0)), full(norm_mix), full(w_uvk), full(w_qv_t),
                  full(ln_g), full(ln_b), full(w_s), full(b_s_t)],
        out_specs=[row_spec, row_spec, t_spec(wide), t_spec(vt_rows)],
        out_shape=[row_sds, row_sds, t_sds(wide), t_sds(vt_rows)],
        compiler_params=_cparams("parallel"),
        name="inproj_sgu",
    )(x2d, norm_mix, w_uvk, w_qv_t, ln_g, ln_b, w_s, b_s_t)


def _diff_attn_kernel(qt_ref, k_ref, vt_ref, lamv_ref, sub_ref, o_ref, kb_scr, acc_scr, s0_scr, s1_scr):
    dqk = DIFF_QK_DIM
    tq = qt_ref.shape[2]
    tk = kb_scr.shape[1]
    h = pl.program_id(1)
    iq = pl.program_id(2)
    slope = jnp.exp2(-(8.0 / DIFF_HEADS) * jnp.full((1, 1), h + 1, jnp.int32).astype(F32))

    qt = qt_ref[0]
    frow = lax.broadcasted_iota(jnp.int32, qt.shape, 0)
    zero = jnp.zeros_like(qt)
    qw = jnp.concatenate([jnp.where(frow < dqk, qt, zero), jnp.where(frow >= dqk, qt, zero)], axis=1)

    slope2 = slope * LOG2E
    nvar = tk // tq

    @pl.when(iq == 0)
    def _():
        krow = lax.broadcasted_iota(jnp.int32, (tk, 2 * tq), 0)
        qcol = lax.broadcasted_iota(jnp.int32, (tk, 2 * tq), 1) % tq
        kb = slope2 * krow.astype(F32)
        kb_scr[0] = kb
        for v in range(nvar):
            kb_scr[1 + v] = jnp.where(krow - v * tq <= qcol, kb, -jnp.inf)

    acc_scr[...] = jnp.zeros(acc_scr.shape, F32)
    q0 = iq * tq
    n_full = q0 // tk
    last_bias = 1 + iq % nvar
    n_loop = n_full // 2

    def block_offset(j):
        off = pl.multiple_of(j * tk, tk)
        return off, slope2 * (off - q0).astype(F32)

    def scores(j, buf):
        off, cj = block_offset(j)
        sel = jnp.where(j == n_full, last_bias, 0)
        s = jnp.dot(k_ref[0, pl.ds(off, tk), :], qw, preferred_element_type=F32) + kb_scr[sel]
        buf[...] = s
        return jnp.max(s, axis=0, keepdims=True) + cj

    def accumulate(j, buf, bm, m_prev):
        off, cj = block_offset(j)
        m_new = jnp.maximum(m_prev, bm)
        p = jnp.exp2(buf[...] - (m_new - cj))
        alpha = jnp.exp2(m_prev - m_new)
        pv = jnp.dot(vt_ref[0, :, pl.ds(off, tk)], p.astype(BF16), preferred_element_type=F32)
        acc_scr[...] = alpha * acc_scr[...] + pv
        return m_new

    def body(t, c):
        m, bm0 = c
        bm1 = scores(2 * t + 1, s1_scr)
        m = accumulate(2 * t, s0_scr, bm0, m)
        bm0 = scores(2 * t + 2, s0_scr)
        m = accumulate(2 * t + 1, s1_scr, bm1, m)
        return m, bm0

    m0 = jnp.full((1, 2 * tq), -jnp.inf, F32)
    bm0 = scores(0, s0_scr)
    m, bm0 = lax.fori_loop(0, n_loop, body, (m0, bm0))
    rest = 2 * n_loop

    @pl.when(n_full > rest)
    def _():
        bm1 = scores(rest + 1, s1_scr)
        m1 = accumulate(rest, s0_scr, bm0, m)
        accumulate(rest + 1, s1_scr, bm1, m1)

    @pl.when(n_full == rest)
    def _():
        accumulate(rest, s0_scr, bm0, m)

    lam = (jnp.exp(jnp.sum(lamv_ref[0:1, :] * lamv_ref[1:2, :], axis=-1, keepdims=True))
           - jnp.exp(jnp.sum(lamv_ref[2:3, :] * lamv_ref[3:4, :], axis=-1, keepdims=True))
           + LAMBDA_INIT)
    dv = DIFF_V_DIM
    on = acc_scr[:dv, :] * (1.0 / acc_scr[dv:dv + 1, :])
    o = on[:, :tq] - lam * on[:, tq:]
    ms = jnp.mean(o * o, axis=0, keepdims=True)
    o = o * lax.rsqrt(ms + RMS_EPS) * sub_ref[...] * (1.0 - LAMBDA_INIT)
    o_ref[0] = o.T.astype(o_ref.dtype)


def _diff_attn(qt, k, vt, lamv, subln_col):
    b, s, _ = k.shape
    tq, tk = ATTN_TQ, ATTN_TK
    dv = DIFF_V_DIM
    assert tk % tq == 0 and s % tk == 0
    return pl.pallas_call(
        _diff_attn_kernel,
        grid=(b, DIFF_HEADS, s // tq),
        in_specs=[pl.BlockSpec((1, 2 * DIFF_QK_DIM, tq), lambda bi, hi, qi: (bi, hi, qi)),
                  pl.BlockSpec((1, s, 2 * DIFF_QK_DIM), lambda bi, hi, qi: (bi, 0, hi)),
                  pl.BlockSpec((1, dv + ONES_ROWS, s), lambda bi, hi, qi: (bi, hi, 0)),
                  pl.BlockSpec(lamv.shape, lambda bi, hi, qi: (0, 0)),
                  pl.BlockSpec(subln_col.shape, lambda bi, hi, qi: (0, 0))],
        out_specs=pl.BlockSpec((1, tq, dv), lambda bi, hi, qi: (bi, qi, hi)),
        out_shape=jax.ShapeDtypeStruct((b, s, DIFF_HEADS * dv), BF16),
        scratch_shapes=[pltpu.VMEM((1 + tk // tq, tk, 2 * tq), F32),
                        pltpu.VMEM((dv + ONES_ROWS, 2 * tq), F32),
                        pltpu.VMEM((tk, 2 * tq), F32), pltpu.VMEM((tk, 2 * tq), F32)],
        compiler_params=_cparams("arbitrary", "arbitrary", "arbitrary"),
        name="diff_attn",
    )(qt, k, vt, lamv, subln_col)


def _mem_kv_kernel(mem_ref, g_ref, w_ref, k_ref, v_ref):
    d = mem_ref.shape[-1]
    mn = _rms(mem_ref[0], g_ref[...]).astype(BF16)
    k_ref[0] = jnp.dot(mn, w_ref[:, :d], preferred_element_type=F32).astype(BF16)
    v_ref[0] = jnp.dot(mn, w_ref[:, d:], preferred_element_type=F32).astype(BF16)


def _mem_kv(mem, norm_mem, w_kv):
    b, m, d = mem.shape
    spec = pl.BlockSpec((1, m, d), lambda i: (i, 0, 0))
    sds = jax.ShapeDtypeStruct((b, m, d), BF16)
    return pl.pallas_call(
        _mem_kv_kernel,
        grid=(b,),
        in_specs=[spec, pl.BlockSpec(norm_mem.shape, lambda i: (0, 0)),
                  pl.BlockSpec(w_kv.shape, lambda i: (0, 0))],
        out_specs=[spec, spec],
        out_shape=[sds, sds],
        compiler_params=_cparams("parallel"),
        name="mem_kv",
    )(mem, norm_mem, w_kv)


def _pack_rows(x):
    w = x.shape[1] // 2
    bits = pltpu.bitcast(x.astype(BF16).astype(F32), jnp.uint32)
    return (bits[:, :w] >> 16) | bits[:, w:]


def _unpack_rows(u):
    lo = pltpu.bitcast(u << 16, F32).astype(BF16)
    hi = pltpu.bitcast(u & jnp.uint32(0xFFFF0000), F32).astype(BF16)
    return lo, hi


def _postmix_kernel(x_ref, a_ref, o_ref, wout_ref, gq_ref, wq_ref, km_ref, vm_ref, wo_ref,
                    gf_ref, wrh_ref, wrl_ref, br_ref, x2_ref, h3_ref, route_ref, counts_ref,
                    ltri_scr, run_scr):
    tm, d = x_ref.shape

    @pl.when(pl.program_id(0) == 0)
    def _():
        r = lax.broadcasted_iota(jnp.int32, (tm, tm), 0)
        c = lax.broadcasted_iota(jnp.int32, (tm, tm), 1)
        ltri_scr[...] = jnp.where(c < r, 1.0, 0.0).astype(BF16)
        run_scr[...] = jnp.zeros(run_scr.shape, F32)

    half = a_ref.shape[1]
    hd = d // MEM_HEADS
    mixed = (jnp.dot(a_ref[...], wout_ref[:half, :], preferred_element_type=F32)
             + jnp.dot(o_ref[...], wout_ref[half:, :], preferred_element_type=F32))
    x1 = x_ref[...] + mixed

    hq = _rms(x1, gq_ref[...]).astype(BF16)
    q = (jnp.dot(hq, wq_ref[...], preferred_element_type=F32) * (hd ** -0.5)).astype(BF16)
    heads = []
    for hh in range(MEM_HEADS):
        cs = slice(hh * hd, (hh + 1) * hd)
        s = lax.dot_general(q[:, cs], km_ref[0, :, cs], (((1,), (1,)), ((), ())),
                            preferred_element_type=F32)
        e = jnp.exp(s - jnp.max(s, axis=-1, keepdims=True))
        p = e / jnp.sum(e, axis=-1, keepdims=True)
        heads.append(jnp.dot(p.astype(BF16), vm_ref[0, :, cs], preferred_element_type=F32).astype(BF16))
    x2 = x1 + jnp.dot(jnp.concatenate(heads, axis=1), wo_ref[...], preferred_element_type=F32)
    x2_ref[...] = x2

    h3 = _rms(x2, gf_ref[...])
    h3_hi = h3.astype(BF16)
    h3_lo = (h3 - h3_hi.astype(F32)).astype(BF16)
    lg = (jnp.dot(h3_hi, wrh_ref[...], preferred_element_type=F32)
          + jnp.dot(h3_lo, wrh_ref[...], preferred_element_type=F32)
          + jnp.dot(h3_hi, wrl_ref[...], preferred_element_type=F32)) + br_ref[...]

    lane = lax.broadcasted_iota(jnp.int32, lg.shape, 1)
    lanef = lane.astype(F32)
    big = float(LANES)
    ninf = -jnp.inf
    gl = jnp.where(lane < N_GROUPS, lg, ninf)
    gmax = jnp.max(gl, axis=-1, keepdims=True)
    gidx = jnp.min(jnp.where(gl == gmax, lanef, big), axis=-1, keepdims=True)
    gate = 1.0 / jnp.sum(jnp.exp(gl - gmax), axis=-1, keepdims=True)
    lo = N_GROUPS + EXPERTS_PER_GROUP * gidx
    el = jnp.where((lanef >= lo) & (lanef < lo + EXPERTS_PER_GROUP), lg, ninf)
    e1 = jnp.max(el, axis=-1, keepdims=True)
    i1 = jnp.min(jnp.where(el == e1, lanef, big), axis=-1, keepdims=True)
    el2 = jnp.where(lanef == i1, ninf, el)
    e2 = jnp.max(el2, axis=-1, keepdims=True)
    i2 = jnp.min(jnp.where(el2 == e2, lanef, big), axis=-1, keepdims=True)
    tt = jnp.exp(e2 - e1)
    w1 = gate / (1.0 + tt)
    w2 = gate * tt / (1.0 + tt)

    e_lo = jnp.minimum(i1, i2) - lo
    e_hi = jnp.maximum(i1, i2) - lo
    pair = e_lo * (2 * EXPERTS_PER_GROUP - 1 - e_lo) * 0.5 + (e_hi - e_lo - 1.0)
    bucket = gidx * PAIRS_PER_GROUP + pair
    first_is_lo = i1 < i2
    w_lo = jnp.where(first_is_lo, w1, w2)
    w_hi = jnp.where(first_is_lo, w2, w1)

    hit = lanef == bucket
    cnt = jnp.where(hit, 1.0, 0.0)
    run = run_scr[0:1, :]
    before = jnp.dot(ltri_scr[...], cnt.astype(BF16), preferred_element_type=F32) + run
    rank = jnp.sum(jnp.where(hit, before, 0.0), axis=-1, keepdims=True)
    run_scr[...] = jnp.broadcast_to(run + jnp.sum(cnt, axis=0, keepdims=True), run_scr.shape)
    counts_ref[...] = run_scr[...]

    slab = jnp.zeros(lg.shape, F32)
    for li, val in enumerate((bucket, rank)):
        slab = jnp.where(lane == li, val, slab)
    route_ref[...] = slab

    gates = jnp.where(lane == 0, w_lo, jnp.where(lane == 1, w_hi, 0.0))
    h3_ref[...] = jnp.concatenate([_pack_rows(h3), pltpu.bitcast(gates, jnp.uint32)], axis=1)


def _postmix(x2d, a_out, o_attn, w_out, norm_xq, w_q, k_mem, v_mem, w_o, norm_ffn,
             wr_hi, wr_lo, b_r, seq):
    n, d = x2d.shape
    tm = TOKEN_TILE
    per_b = seq // tm
    half = a_out.shape[1]
    row = lambda w: pl.BlockSpec((tm, w), lambda i: (i, 0))
    full = lambda a: pl.BlockSpec(a.shape, lambda i: (0,) * a.ndim)
    memspec = pl.BlockSpec((1,) + k_mem.shape[1:], lambda i: (i // per_b, 0, 0))
    return pl.pallas_call(
        _postmix_kernel,
        grid=(n // tm,),
        in_specs=[row(d), row(half), row(half), full(w_out), full(norm_xq), full(w_q),
                  memspec, memspec, full(w_o), full(norm_ffn), full(wr_hi), full(wr_lo), full(b_r)],
        out_specs=[row(d), row(d // 2 + LANES), row(LANES), pl.BlockSpec((8, LANES), lambda i: (0, 0))],
        out_shape=[jax.ShapeDtypeStruct((n, d), F32), jax.ShapeDtypeStruct((n, d // 2 + LANES), jnp.uint32),
                   jax.ShapeDtypeStruct((n, LANES), F32), jax.ShapeDtypeStruct((8, LANES), F32)],
        scratch_shapes=[pltpu.VMEM((tm, tm), BF16), pltpu.VMEM((8, LANES), F32)],
        compiler_params=_cparams("arbitrary"),
        name="postmix",
    )(x2d, a_out, o_attn, w_out, norm_xq, w_q, k_mem, v_mem, w_o, norm_ffn, wr_hi, wr_lo, b_r)


def _drain(copy, count):
    def body(t, carry):
        copy.wait()
        return carry
    lax.fori_loop(0, count, body, 0, unroll=8)


def _dispatch_kernel(pos_ref, h_ref, xin_ref, xout_ref, sem):
    del xin_ref
    groups = h_ref.shape[0]

    def row_copy(g, j, p):
        return pltpu.make_async_copy(h_ref.at[g, pl.ds(j, 1), :], xout_ref.at[pl.ds(p, 1), :], sem)

    def issue(g, carry):
        for j in range(SUBLANES):
            row_copy(g, j, pos_ref[SUBLANES * g + j]).start(priority=j % 2)
        return carry

    lax.fori_loop(0, groups, issue, 0)
    _drain(row_copy(0, 0, 0), groups * SUBLANES)


def _dispatch(pos, h3p, n_rows):
    n, w = h3p.shape
    tm = MOE_TILE
    x0 = jnp.zeros((n_rows, w), jnp.uint32)
    return pl.pallas_call(
        _dispatch_kernel,
        grid=(n // tm,),
        in_specs=[pl.BlockSpec((tm,), lambda i: (i,), memory_space=pltpu.SMEM),
                  pl.BlockSpec((tm // SUBLANES, SUBLANES, w), lambda i: (i, 0, 0)),
                  pl.BlockSpec(memory_space=pl.ANY)],
        out_specs=pl.BlockSpec(memory_space=pl.ANY),
        out_shape=jax.ShapeDtypeStruct((n_rows, w), jnp.uint32),
        scratch_shapes=[pltpu.SemaphoreType.DMA(())],
        input_output_aliases={2: 0},
        compiler_params=_cparams("arbitrary"),
        name="dispatch",
    )(pos, h3p.reshape(n // SUBLANES, SUBLANES, w), x0)


def _expert_kernel(ea_ref, eb_ref, nused_ref, x_ref, wga_ref, wua_ref, wda_ref, wgb_ref, wub_ref, wdb_ref,
                   y_ref):
    del ea_ref, eb_ref
    i = pl.program_id(0)
    half = y_ref.shape[1]

    @pl.when(i < nused_ref[0])
    def _():
        x_lo, x_hi = _unpack_rows(x_ref[:, :half])
        gates = pltpu.bitcast(x_ref[:, half:], F32)

        def expert(wg_ref, wu_ref, wd_ref):
            def proj(w_ref):
                return (jnp.dot(x_lo, w_ref[0, :half, :], preferred_element_type=F32)
                        + jnp.dot(x_hi, w_ref[0, half:, :], preferred_element_type=F32))
            hmid = (jax.nn.silu(proj(wg_ref)) * proj(wu_ref)).astype(BF16)
            return jnp.dot(hmid, wd_ref[0], preferred_element_type=F32)

        y = (gates[:, 0:1] * expert(wga_ref, wua_ref, wda_ref)
             + gates[:, 1:2] * expert(wgb_ref, wub_ref, wdb_ref))
        y_ref[...] = _pack_rows(y)

    @pl.when(i >= nused_ref[0])
    def _():
        y_ref[...] = jnp.zeros(y_ref.shape, y_ref.dtype)


def _experts(block_ea, block_eb, n_used, x_rows, w_gate, w_up, w_down):
    p, w = x_rows.shape
    tb = EXPERT_ROWS
    _, d, de = w_gate.shape
    up = lambda sel: pl.BlockSpec((1, d, de), lambda i, ea, eb, nu: (sel(ea, eb)[i], 0, 0))
    down = lambda sel: pl.BlockSpec((1, de, d), lambda i, ea, eb, nu: (sel(ea, eb)[i], 0, 0))
    first = lambda ea, eb: ea
    second = lambda ea, eb: eb
    grid_spec = pltpu.PrefetchScalarGridSpec(
        num_scalar_prefetch=3,
        grid=(p // tb,),
        in_specs=[pl.BlockSpec((tb, w), lambda i, ea, eb, nu: (i, 0)),
                  up(first), up(first), down(first), up(second), up(second), down(second)],
        out_specs=pl.BlockSpec((tb, d // 2), lambda i, ea, eb, nu: (i, 0)),
    )
    return pl.pallas_call(
        _expert_kernel,
        grid_spec=grid_spec,
        out_shape=jax.ShapeDtypeStruct((p, d // 2), jnp.uint32),
        compiler_params=_cparams("arbitrary"),
        name="experts",
    )(block_ea, block_eb, n_used, x_rows, w_gate, w_up, w_down, w_gate, w_up, w_down)


def _combine_kernel(pos_ref, posn_ref, x2_ref, g_ref, y_hbm, o_ref, ybuf, sems):
    i = pl.program_id(0)
    tm, d = x2_ref.shape
    half = d // 2
    groups = tm // SUBLANES
    slot = i % 2

    def row_copy(p, s, g, j):
        return pltpu.make_async_copy(y_hbm.at[pl.ds(p, 1), :], ybuf.at[s, g, pl.ds(j, 1), :], sems.at[s])

    def issue(p_ref, s):
        def body(g, carry):
            for j in range(SUBLANES):
                row_copy(p_ref[SUBLANES * g + j], s, g, j).start(priority=j % 2)
            return carry
        lax.fori_loop(0, groups, body, 0)

    @pl.when(i == 0)
    def _():
        issue(pos_ref, 0)

    @pl.when(i + 1 < pl.num_programs(0))
    def _():
        issue(posn_ref, 1 - slot)

    _drain(row_copy(0, slot, 0, 0), tm)

    y_lo, y_hi = _unpack_rows(ybuf[slot].reshape(tm, half))
    xl = x2_ref[:, :half] + y_lo.astype(F32)
    xh = x2_ref[:, half:] + y_hi.astype(F32)
    ms = (jnp.sum(xl * xl, axis=-1, keepdims=True) + jnp.sum(xh * xh, axis=-1, keepdims=True)) / d
    inv = lax.rsqrt(ms + RMS_EPS)
    o_ref[:, :half] = xl * inv * g_ref[:, :half]
    o_ref[:, half:] = xh * inv * g_ref[:, half:]


def _combine(pos, x2, norm_final, y_rows):
    n, d = x2.shape
    tm = MOE_TILE
    steps = n // tm
    row = lambda w: pl.BlockSpec((tm, w), lambda i: (i, 0))
    pos_spec = lambda f: pl.BlockSpec((tm,), f, memory_space=pltpu.SMEM)
    return pl.pallas_call(
        _combine_kernel,
        grid=(steps,),
        in_specs=[pos_spec(lambda i: (i,)), pos_spec(lambda i: (jnp.minimum(i + 1, steps - 1),)),
                  row(d), pl.BlockSpec(norm_final.shape, lambda i: (0, 0)),
                  pl.BlockSpec(memory_space=pl.ANY)],
        out_specs=row(d),
        out_shape=jax.ShapeDtypeStruct((n, d), F32),
        scratch_shapes=[pltpu.VMEM((2, tm // SUBLANES, SUBLANES, d // 2), jnp.uint32),
                        pltpu.SemaphoreType.DMA((2,))],
        compiler_params=_cparams("arbitrary"),
        name="combine",
    )(pos, pos, x2, norm_final, y_rows)


def _bucket_experts():
    ea, eb = [], []
    for g in range(N_GROUPS):
        for a in range(EXPERTS_PER_GROUP):
            for b in range(a + 1, EXPERTS_PER_GROUP):
                ea.append(g * EXPERTS_PER_GROUP + a)
                eb.append(g * EXPERTS_PER_GROUP + b)
    return jnp.asarray(ea, jnp.int32), jnp.asarray(eb, jnp.int32)


def _dispatch_plan(route, counts_slab, n_rows):
    tb = EXPERT_ROWS
    counts = counts_slab[0, :N_BUCKETS].astype(jnp.int32)
    padded = (counts + tb - 1) // tb * tb
    pend = jnp.cumsum(padded)
    pstart = pend - padded
    bucket = route[:, 0].astype(jnp.int32)
    rank = route[:, 1].astype(jnp.int32)
    hit = bucket[:, None] == jnp.arange(N_BUCKETS, dtype=jnp.int32)[None, :]
    pos = jnp.sum(jnp.where(hit, pstart[None, :], 0), axis=1) + rank
    nb = n_rows // tb
    block_start = jnp.arange(nb, dtype=jnp.int32) * tb
    block_bucket = jnp.minimum(jnp.sum((pend[None, :] <= block_start[:, None]).astype(jnp.int32), axis=1),
                               N_BUCKETS - 1)
    ea, eb = _bucket_experts()
    n_used = (pend[-1] // tb).astype(jnp.int32).reshape(1)
    return pos, jnp.take(ea, block_bucket), jnp.take(eb, block_bucket), n_used


def kernel(x, mem, norm_mix, w_in, sgu_ln_g, sgu_ln_b, sgu_w, sgu_b, lambda_q1, lambda_k1, lambda_q2,
           lambda_k2, diff_subln, w_out, norm_xq, norm_mem, w_q_mem, w_kv_mem, w_o_mem, norm_ffn,
           w_router_group, b_router_group, w_router_expert, b_router_expert, w_gate, w_up, w_down,
           norm_final):
    bn, sn, d = x.shape
    n = bn * sn
    assert w_in.shape[0] == 1 and sn % TOKEN_TILE == 0 and sn % ATTN_TK == 0 and n % MOE_TILE == 0
    l = 0
    x2d = x.reshape(n, d)

    sw = SGU_GROUPS * SGU_CH
    qkw = DIFF_HEADS * 2 * DIFF_QK_DIM
    w_in_l = w_in[l].astype(BF16)
    w_uvk = jnp.concatenate([w_in_l[:, :2 * sw], w_in_l[:, 2 * sw + qkw:2 * sw + 2 * qkw]], axis=1)
    w_qv_t = jnp.concatenate([w_in_l[:, 2 * sw:2 * sw + qkw], w_in_l[:, 2 * sw + 2 * qkw:]], axis=1).T
    a_out, k, qt, vt = _inproj_sgu(x2d, norm_mix[l][None], w_uvk, w_qv_t, sgu_ln_g[l], sgu_ln_b[l],
                                   sgu_w[l], sgu_b[l].T, sn)
    lamv = jnp.stack([lambda_q1[l], lambda_k1[l], lambda_q2[l], lambda_k2[l]])
    wide = k.shape[1]
    o_attn = _diff_attn(qt, k.reshape(bn, sn, wide), vt, lamv, diff_subln[l][:, None]).reshape(n, wide)
    k_mem, v_mem = _mem_kv(mem, norm_mem[l][None], w_kv_mem[l].astype(BF16))

    w_r = jnp.concatenate([w_router_group[l], w_router_expert[l]], axis=1)
    w_r = jnp.pad(w_r, ((0, 0), (0, LANES - w_r.shape[1])))
    wr_hi = w_r.astype(BF16)
    wr_lo = (w_r - wr_hi.astype(F32)).astype(BF16)
    b_r = jnp.concatenate([b_router_group[l], b_router_expert[l]])
    b_r = jnp.pad(b_r, (0, LANES - b_r.shape[0]))[None]
    x2, h3p, route, counts = _postmix(x2d, a_out, o_attn, w_out[l].astype(BF16), norm_xq[l][None],
                                      w_q_mem[l].astype(BF16), k_mem, v_mem, w_o_mem[l].astype(BF16),
                                      norm_ffn[l][None], wr_hi, wr_lo, b_r, sn)

    n_rows = n + N_BUCKETS * EXPERT_ROWS
    pos, block_ea, block_eb, n_used = _dispatch_plan(route, counts, n_rows)
    x_rows = _dispatch(pos, h3p, n_rows)
    y_rows = _experts(block_ea, block_eb, n_used, x_rows, w_gate[l].astype(BF16), w_up[l].astype(BF16),
                      w_down[l].astype(BF16))
    out = _combine(pos, x2, norm_final[None], y_rows)
    return out.reshape(bn, sn, d)
```

```python
import functools
import math

import jax
import jax.numpy as jnp
from jax import lax
from jax.experimental import pallas as pl
from jax.experimental.pallas import tpu as pltpu

F32 = jnp.float32
BF16 = jnp.bfloat16

LANES = 128
SGU_GROUPS = 4
SGU_CH = 128
CHUNK = 128
DIFF_HEADS = 4
DIFF_QK_DIM = 64
DIFF_V_DIM = 128
MEM_HEADS = 4
N_GROUPS = 4
EXPERTS_PER_GROUP = 8
N_EXPERTS = N_GROUPS * EXPERTS_PER_GROUP
PAIRS_PER_GROUP = EXPERTS_PER_GROUP * (EXPERTS_PER_GROUP - 1) // 2
N_BUCKETS = N_GROUPS * PAIRS_PER_GROUP
TOP_K = 2
RMS_EPS = 1e-6
LN_EPS = 1e-5
LOG2E = math.log2(math.e)
LAMBDA_INIT = 0.8 - 0.6 * math.exp(-0.3 * 0)

TOKEN_TILE = 512
ATTN_TQ = 512
ATTN_TK = 512
SLOPE_PARTS = 3
ONES_ROWS = 16
MOE_TILE = 1024
EXPERT_ROWS = 256
SUBLANES = 8
VMEM_LIMIT = 56 * 1024 * 1024


def _cparams(*sem):
    return pltpu.CompilerParams(dimension_semantics=sem, vmem_limit_bytes=VMEM_LIMIT)


def _gelu(x):
    return 0.5 * x * (1.0 + lax.erf(x * (2.0 ** -0.5)))


def _rms(x, g):
    ms = jnp.mean(x * x, axis=-1, keepdims=True)
    return x * lax.rsqrt(ms + RMS_EPS) * g


def _inproj_sgu_kernel(x_ref, g_ref, w_ref, wt_ref, lng_ref, lnb_ref, ws_ref, bs_ref,
                       a_ref, k_ref, qt_ref, vt_ref, *, tiles_per_seq):
    tm = x_ref.shape[0]
    h = _rms(x_ref[...], g_ref[...]).astype(BF16)
    sw = SGU_GROUPS * SGU_CH
    qkw = DIFF_HEADS * 2 * DIFF_QK_DIM

    def proj(c0, c1):
        return jnp.dot(h, w_ref[:, c0:c1], preferred_element_type=F32)

    def proj_t(r0, r1):
        return lax.dot_general(wt_ref[r0:r1, :], h, (((1,), (1,)), ((), ())), preferred_element_type=F32)

    u = _gelu(proj(0, sw))
    v = _gelu(proj(sw, 2 * sw))
    kf = proj(2 * sw, 2 * sw + qkw)
    pos = (pl.program_id(0) % tiles_per_seq) * tm + lax.broadcasted_iota(jnp.int32, (tm, LANES), 0)
    lane = lax.broadcasted_iota(jnp.int32, (tm, LANES), 1)
    pos_hi = ((pos >> 8) << 8).astype(F32)
    pos_lo = (pos & 255).astype(F32)
    aug = jnp.where(lane < SLOPE_PARTS, pos_hi, jnp.where(lane < 2 * SLOPE_PARTS, pos_lo, 0.0))
    pieces = []
    for hh in range(DIFF_HEADS):
        pieces += [kf[:, hh * 2 * DIFF_QK_DIM:(hh + 1) * 2 * DIFF_QK_DIM], aug]
    k_ref[...] = jnp.concatenate(pieces, axis=1).astype(BF16)
    qt_ref[0] = (proj_t(0, qkw) * (DIFF_QK_DIM ** -0.5 * LOG2E)).astype(BF16)
    vt = proj_t(qkw, qkw + DIFF_HEADS * DIFF_V_DIM)
    ones = jnp.ones((ONES_ROWS, tm), F32)
    pieces = []
    for hh in range(DIFF_HEADS):
        pieces += [vt[hh * DIFF_V_DIM:(hh + 1) * DIFF_V_DIM, :], ones]
    vt_ref[0] = jnp.concatenate(pieces, axis=0).astype(BF16)

    row = lax.broadcasted_iota(jnp.int32, (CHUNK, CHUNK), 0)
    col = lax.broadcasted_iota(jnp.int32, (CHUNK, CHUNK), 1)
    nchunk = tm // CHUNK
    for g in range(SGU_GROUPS):
        cs = slice(g * SGU_CH, (g + 1) * SGU_CH)
        vg = v[:, cs]
        mu = jnp.mean(vg, axis=-1, keepdims=True)
        d = vg - mu
        var = jnp.mean(d * d, axis=-1, keepdims=True)
        vn = (d * lax.rsqrt(var + LN_EPS) * lng_ref[g:g + 1, :] + lnb_ref[g:g + 1, :]).astype(BF16)
        rhs = jnp.concatenate([vn[c * CHUNK:(c + 1) * CHUNK, :] for c in range(nchunk)], axis=1)
        wt = jnp.where(row >= col, ws_ref[g], 0.0).astype(BF16)
        sp = jnp.dot(wt, rhs, preferred_element_type=F32) + bs_ref[:, g:g + 1]
        for c in range(nchunk):
            rs = slice(c * CHUNK, (c + 1) * CHUNK)
            a_ref[rs, cs] = (u[rs, cs] * sp[:, c * SGU_CH:(c + 1) * SGU_CH]).astype(BF16)


def _inproj_sgu(x2d, norm_mix, w_uvk, w_qv_t, ln_g, ln_b, w_s, b_s_t, seq):
    n, d = x2d.shape
    tm = TOKEN_TILE
    per_b = seq // tm
    wide = DIFF_HEADS * DIFF_V_DIM
    k_cols = DIFF_HEADS * (2 * DIFF_QK_DIM + LANES)
    row_spec = pl.BlockSpec((tm, wide), lambda i: (i, 0))
    vt_rows = DIFF_HEADS * (DIFF_V_DIM + ONES_ROWS)
    t_spec = lambda rows: pl.BlockSpec((1, rows, tm), lambda i: (i // per_b, 0, i % per_b))
    full = lambda a: pl.BlockSpec(a.shape, lambda i: (0,) * a.ndim)
    row_sds = jax.ShapeDtypeStruct((n, wide), BF16)
    t_sds = lambda rows: jax.ShapeDtypeStruct((n // seq, rows, seq), BF16)
    return pl.pallas_call(
        functools.partial(_inproj_sgu_kernel, tiles_per_seq=per_b),
        grid=(n // tm,),
        in_specs=[pl.BlockSpec((tm, d), lambda i: (i, 0)), full(norm_mix), full(w_uvk), full(w_qv_t),
                  full(ln_g), full(ln_b), full(w_s), full(b_s_t)],
        out_specs=[row_spec, pl.BlockSpec((tm, k_cols), lambda i: (i, 0)), t_spec(wide), t_spec(vt_rows)],
        out_shape=[row_sds, jax.ShapeDtypeStruct((n, k_cols), BF16), t_sds(wide), t_sds(vt_rows)],
        compiler_params=_cparams("parallel"),
        name="inproj_sgu",
    )(x2d, norm_mix, w_uvk, w_qv_t, ln_g, ln_b, w_s, b_s_t)


def _diff_attn_kernel(qt_ref, k_ref, vt_ref, lamv_ref, sub_ref, o_ref, mask_scr, acc_scr, s0_scr, s1_scr):
    dqk = DIFF_QK_DIM
    tq = qt_ref.shape[2]
    tk = mask_scr.shape[1]
    h = pl.program_id(1)
    iq = pl.program_id(2)
    slope2 = LOG2E * jnp.exp2(-(8.0 / DIFF_HEADS) * jnp.full((1, 1), h + 1, jnp.int32).astype(F32))

    qt = qt_ref[0]
    frow = lax.broadcasted_iota(jnp.int32, qt.shape, 0)
    zero = jnp.zeros_like(qt)
    qw = jnp.concatenate([jnp.where(frow < dqk, qt, zero), jnp.where(frow >= dqk, qt, zero)], axis=1)
    arow = lax.broadcasted_iota(jnp.int32, (LANES, 2 * tq), 0)
    coef = jnp.zeros((LANES, 2 * tq), F32)
    rem = slope2
    for i in range(SLOPE_PARTS):
        piece = rem.astype(BF16).astype(F32)
        rem = rem - piece
        coef = jnp.where(arow == i, piece, jnp.where(arow == SLOPE_PARTS + i, piece, coef))
    qw = jnp.concatenate([qw, coef.astype(BF16)], axis=0)
    nvar = tk // tq

    @pl.when(iq == 0)
    def _():
        krow = lax.broadcasted_iota(jnp.int32, (tk, 2 * tq), 0)
        qcol = lax.broadcasted_iota(jnp.int32, (tk, 2 * tq), 1) % tq
        mask_scr[0] = jnp.zeros((tk, 2 * tq), F32)
        for v in range(nvar):
            mask_scr[1 + v] = jnp.where(krow - v * tq <= qcol, 0.0, -jnp.inf)

    acc_scr[...] = jnp.zeros(acc_scr.shape, F32)
    n_full = (iq * tq) // tk
    last_mask = 1 + iq % nvar

    def scores(j, buf, mask_sel=None):
        off = pl.multiple_of(j * tk, tk)
        s = jnp.dot(k_ref[0, pl.ds(off, tk), :], qw, preferred_element_type=F32)
        if mask_sel is not None:
            s = s + mask_scr[mask_sel]
        buf[...] = s
        return jnp.max(s, axis=0, keepdims=True)

    def accumulate(j, buf, bm, m_prev):
        off = pl.multiple_of(j * tk, tk)
        m_new = jnp.maximum(m_prev, bm)
        p = jnp.exp2(buf[...] - m_new)
        alpha = jnp.exp2(m_prev - m_new)
        pv = jnp.dot(vt_ref[0, :, pl.ds(off, tk)], p.astype(BF16), preferred_element_type=F32)
        acc_scr[...] = alpha * acc_scr[...] + pv
        return m_new

    def body(t, c):
        m, bm0 = c
        bm1 = scores(2 * t + 1, s1_scr)
        m = accumulate(2 * t, s0_scr, bm0, m)
        bm0 = scores(2 * t + 2, s0_scr)
        m = accumulate(2 * t + 1, s1_scr, bm1, m)
        return m, bm0

    m0 = jnp.full((1, 2 * tq), -jnp.inf, F32)
    bm0 = scores(0, s0_scr, jnp.where(n_full == 0, last_mask, 0))
    n_loop = jnp.maximum(n_full - 1, 0) // 2
    m, bm0 = lax.fori_loop(0, n_loop, body, (m0, bm0))
    rest = 2 * n_loop
    left = n_full - rest

    @pl.when(left == 0)
    def _():
        accumulate(rest, s0_scr, bm0, m)

    @pl.when(left == 1)
    def _():
        bm1 = scores(rest + 1, s1_scr, last_mask)
        m1 = accumulate(rest, s0_scr, bm0, m)
        accumulate(rest + 1, s1_scr, bm1, m1)

    @pl.when(left == 2)
    def _():
        bm1 = scores(rest + 1, s1_scr)
        m1 = accumulate(rest, s0_scr, bm0, m)
        bm2 = scores(rest + 2, s0_scr, last_mask)
        m2 = accumulate(rest + 1, s1_scr, bm1, m1)
        accumulate(rest + 2, s0_scr, bm2, m2)

    lam = (jnp.exp(jnp.sum(lamv_ref[0:1, :] * lamv_ref[1:2, :], axis=-1, keepdims=True))
           - jnp.exp(jnp.sum(lamv_ref[2:3, :] * lamv_ref[3:4, :], axis=-1, keepdims=True))
           + LAMBDA_INIT)
    dv = DIFF_V_DIM
    on = acc_scr[:dv, :] * (1.0 / acc_scr[dv:dv + 1, :])
    o = on[:, :tq] - lam * on[:, tq:]
    ms = jnp.mean(o * o, axis=0, keepdims=True)
    o = o * lax.rsqrt(ms + RMS_EPS) * sub_ref[...] * (1.0 - LAMBDA_INIT)
    o_ref[0] = o.T.astype(o_ref.dtype)


def _diff_attn(qt, k, vt, lamv, subln_col):
    b, s, _ = k.shape
    tq, tk = ATTN_TQ, ATTN_TK
    dv = DIFF_V_DIM
    assert tk % tq == 0 and s % tk == 0
    return pl.pallas_call(
        _diff_attn_kernel,
        grid=(b, DIFF_HEADS, s // tq),
        in_specs=[pl.BlockSpec((1, 2 * DIFF_QK_DIM, tq), lambda bi, hi, qi: (bi, hi, qi)),
                  pl.BlockSpec((1, s, 2 * DIFF_QK_DIM + LANES), lambda bi, hi, qi: (bi, 0, hi)),
                  pl.BlockSpec((1, dv + ONES_ROWS, s), lambda bi, hi, qi: (bi, hi, 0)),
                  pl.BlockSpec(lamv.shape, lambda bi, hi, qi: (0, 0)),
                  pl.BlockSpec(subln_col.shape, lambda bi, hi, qi: (0, 0))],
        out_specs=pl.BlockSpec((1, tq, dv), lambda bi, hi, qi: (bi, qi, hi)),
        out_shape=jax.ShapeDtypeStruct((b, s, DIFF_HEADS * dv), BF16),
        scratch_shapes=[pltpu.VMEM((1 + tk // tq, tk, 2 * tq), F32),
                        pltpu.VMEM((dv + ONES_ROWS, 2 * tq), F32),
                        pltpu.VMEM((tk, 2 * tq), F32), pltpu.VMEM((tk, 2 * tq), F32)],
        compiler_params=_cparams("arbitrary", "arbitrary", "arbitrary"),
        name="diff_attn",
    )(qt, k, vt, lamv, subln_col)


def _mem_kv_kernel(mem_ref, g_ref, w_ref, k_ref, v_ref):
    d = mem_ref.shape[-1]
    mn = _rms(mem_ref[0], g_ref[...]).astype(BF16)
    k_ref[0] = jnp.dot(mn, w_ref[:, :d], preferred_element_type=F32).astype(BF16)
    v_ref[0] = jnp.dot(mn, w_ref[:, d:], preferred_element_type=F32).astype(BF16)


def _mem_kv(mem, norm_mem, w_kv):
    b, m, d = mem.shape
    spec = pl.BlockSpec((1, m, d), lambda i: (i, 0, 0))
    sds = jax.ShapeDtypeStruct((b, m, d), BF16)
    return pl.pallas_call(
        _mem_kv_kernel,
        grid=(b,),
        in_specs=[spec, pl.BlockSpec(norm_mem.shape, lambda i: (0, 0)),
                  pl.BlockSpec(w_kv.shape, lambda i: (0, 0))],
        out_specs=[spec, spec],
        out_shape=[sds, sds],
        compiler_params=_cparams("parallel"),
        name="mem_kv",
    )(mem, norm_mem, w_kv)


def _pack_rows(x):
    w = x.shape[1] // 2
    bits = pltpu.bitcast(x.astype(BF16).astype(F32), jnp.uint32)
    return (bits[:, :w] >> 16) | bits[:, w:]


def _unpack_rows(u):
    lo = pltpu.bitcast(u << 16, F32).astype(BF16)
    hi = pltpu.bitcast(u & jnp.uint32(0xFFFF0000), F32).astype(BF16)
    return lo, hi


def _postmix_kernel(x_ref, a_ref, o_ref, wout_ref, gq_ref, wq_ref, km_ref, vm_ref, wo_ref,
                    gf_ref, wrh_ref, wrl_ref, br_ref, x2_ref, h3_ref, route_ref, counts_ref,
                    ltri_scr, run_scr):
    tm, d = x_ref.shape

    @pl.when(pl.program_id(0) == 0)
    def _():
        r = lax.broadcasted_iota(jnp.int32, (tm, tm), 0)
        c = lax.broadcasted_iota(jnp.int32, (tm, tm), 1)
        ltri_scr[...] = jnp.where(c < r, 1.0, 0.0).astype(BF16)
        run_scr[...] = jnp.zeros(run_scr.shape, F32)

    half = a_ref.shape[1]
    hd = d // MEM_HEADS
    mixed = (jnp.dot(a_ref[...], wout_ref[:half, :], preferred_element_type=F32)
             + jnp.dot(o_ref[...], wout_ref[half:, :], preferred_element_type=F32))
    x1 = x_ref[...] + mixed

    hq = _rms(x1, gq_ref[...]).astype(BF16)
    q = (jnp.dot(hq, wq_ref[...], preferred_element_type=F32) * (hd ** -0.5)).astype(BF16)
    heads = []
    for hh in range(MEM_HEADS):
        cs = slice(hh * hd, (hh + 1) * hd)
        s = lax.dot_general(q[:, cs], km_ref[0, :, cs], (((1,), (1,)), ((), ())),
                            preferred_element_type=F32)
        e = jnp.exp(s - jnp.max(s, axis=-1, keepdims=True))
        p = e / jnp.sum(e, axis=-1, keepdims=True)
        heads.append(jnp.dot(p.astype(BF16), vm_ref[0, :, cs], preferred_element_type=F32).astype(BF16))
    x2 = x1 + jnp.dot(jnp.concatenate(heads, axis=1), wo_ref[...], preferred_element_type=F32)
    x2_ref[...] = x2

    h3 = _rms(x2, gf_ref[...])
    h3_hi = h3.astype(BF16)
    h3_lo = (h3 - h3_hi.astype(F32)).astype(BF16)
    lg = (jnp.dot(h3_hi, wrh_ref[...], preferred_element_type=F32)
          + jnp.dot(h3_lo, wrh_ref[...], preferred_element_type=F32)
          + jnp.dot(h3_hi, wrl_ref[...], preferred_element_type=F32)) + br_ref[...]

    lane = lax.broadcasted_iota(jnp.int32, lg.shape, 1)
    lanef = lane.astype(F32)
    big = float(LANES)
    ninf = -jnp.inf
    gl = jnp.where(lane < N_GROUPS, lg, ninf)
    gmax = jnp.max(gl, axis=-1, keepdims=True)
    gidx = jnp.min(jnp.where(gl == gmax, lanef, big), axis=-1, keepdims=True)
    gate = 1.0 / jnp.sum(jnp.exp(gl - gmax), axis=-1, keepdims=True)
    lo = N_GROUPS + EXPERTS_PER_GROUP * gidx
    el = jnp.where((lanef >= lo) & (lanef < lo + EXPERTS_PER_GROUP), lg, ninf)
    e1 = jnp.max(el, axis=-1, keepdims=True)
    i1 = jnp.min(jnp.where(el == e1, lanef, big), axis=-1, keepdims=True)
    el2 = jnp.where(lanef == i1, ninf, el)
    e2 = jnp.max(el2, axis=-1, keepdims=True)
    i2 = jnp.min(jnp.where(el2 == e2, lanef, big), axis=-1, keepdims=True)
    tt = jnp.exp(e2 - e1)
    w1 = gate / (1.0 + tt)
    w2 = gate * tt / (1.0 + tt)

    e_lo = jnp.minimum(i1, i2) - lo
    e_hi = jnp.maximum(i1, i2) - lo
    pair = e_lo * (2 * EXPERTS_PER_GROUP - 1 - e_lo) * 0.5 + (e_hi - e_lo - 1.0)
    bucket = gidx * PAIRS_PER_GROUP + pair
    first_is_lo = i1 < i2
    w_lo = jnp.where(first_is_lo, w1, w2)
    w_hi = jnp.where(first_is_lo, w2, w1)

    hit = lanef == bucket
    cnt = jnp.where(hit, 1.0, 0.0)
    run = run_scr[0:1, :]
    before = jnp.dot(ltri_scr[...], cnt.astype(BF16), preferred_element_type=F32) + run
    rank = jnp.sum(jnp.where(hit, before, 0.0), axis=-1, keepdims=True)
    run_scr[...] = jnp.broadcast_to(run + jnp.sum(cnt, axis=0, keepdims=True), run_scr.shape)
    counts_ref[...] = run_scr[...]

    slab = jnp.zeros(lg.shape, F32)
    for li, val in enumerate((bucket, rank)):
        slab = jnp.where(lane == li, val, slab)
    route_ref[...] = slab

    gates = jnp.where(lane == 0, w_lo, jnp.where(lane == 1, w_hi, 0.0))
    h3_ref[...] = jnp.concatenate([_pack_rows(h3), pltpu.bitcast(gates, jnp.uint32)], axis=1)


def _postmix(x2d, a_out, o_attn, w_out, norm_xq, w_q, k_mem, v_mem, w_o, norm_ffn,
             wr_hi, wr_lo, b_r, seq):
    n, d = x2d.shape
    tm = TOKEN_TILE
    per_b = seq // tm
    half = a_out.shape[1]
    row = lambda w: pl.BlockSpec((tm, w), lambda i: (i, 0))
    full = lambda a: pl.BlockSpec(a.shape, lambda i: (0,) * a.ndim)
    memspec = pl.BlockSpec((1,) + k_mem.shape[1:], lambda i: (i // per_b, 0, 0))
    return pl.pallas_call(
        _postmix_kernel,
        grid=(n // tm,),
        in_specs=[row(d), row(half), row(half), full(w_out), full(norm_xq), full(w_q),
                  memspec, memspec, full(w_o), full(norm_ffn), full(wr_hi), full(wr_lo), full(b_r)],
        out_specs=[row(d), row(d // 2 + LANES), row(LANES), pl.BlockSpec((8, LANES), lambda i: (0, 0))],
        out_shape=[jax.ShapeDtypeStruct((n, d), F32), jax.ShapeDtypeStruct((n, d // 2 + LANES), jnp.uint32),
                   jax.ShapeDtypeStruct((n, LANES), F32), jax.ShapeDtypeStruct((8, LANES), F32)],
        scratch_shapes=[pltpu.VMEM((tm, tm), BF16), pltpu.VMEM((8, LANES), F32)],
        compiler_params=_cparams("arbitrary"),
        name="postmix",
    )(x2d, a_out, o_attn, w_out, norm_xq, w_q, k_mem, v_mem, w_o, norm_ffn, wr_hi, wr_lo, b_r)


def _drain(copy, count):
    def body(t, carry):
        copy.wait()
        return carry
    lax.fori_loop(0, count, body, 0, unroll=8)


def _dispatch_kernel(pos_ref, h_ref, xin_ref, xout_ref, sem):
    del xin_ref
    groups = h_ref.shape[0]

    def row_copy(g, j, p):
        return pltpu.make_async_copy(h_ref.at[g, pl.ds(j, 1), :], xout_ref.at[pl.ds(p, 1), :], sem)

    def issue(g, carry):
        for j in range(SUBLANES):
            row_copy(g, j, pos_ref[SUBLANES * g + j]).start(priority=j % 2)
        return carry

    lax.fori_loop(0, groups, issue, 0)
    _drain(row_copy(0, 0, 0), groups * SUBLANES)


def _dispatch(pos, h3p, n_rows):
    n, w = h3p.shape
    tm = MOE_TILE
    x0 = jnp.zeros((n_rows, w), jnp.uint32)
    return pl.pallas_call(
        _dispatch_kernel,
        grid=(n // tm,),
        in_specs=[pl.BlockSpec((tm,), lambda i: (i,), memory_space=pltpu.SMEM),
                  pl.BlockSpec((tm // SUBLANES, SUBLANES, w), lambda i: (i, 0, 0)),
                  pl.BlockSpec(memory_space=pl.ANY)],
        out_specs=pl.BlockSpec(memory_space=pl.ANY),
        out_shape=jax.ShapeDtypeStruct((n_rows, w), jnp.uint32),
        scratch_shapes=[pltpu.SemaphoreType.DMA(())],
        input_output_aliases={2: 0},
        compiler_params=_cparams("arbitrary"),
        name="dispatch",
    )(pos, h3p.reshape(n // SUBLANES, SUBLANES, w), x0)


def _expert_kernel(ea_ref, eb_ref, nused_ref, x_ref, wga_ref, wua_ref, wda_ref, wgb_ref, wub_ref, wdb_ref,
                   y_ref):
    del ea_ref, eb_ref
    i = pl.program_id(0)
    half = y_ref.shape[1]

    @pl.when(i < nused_ref[0])
    def _():
        x_lo, x_hi = _unpack_rows(x_ref[:, :half])
        gates = pltpu.bitcast(x_ref[:, half:], F32)

        def expert(wg_ref, wu_ref, wd_ref):
            def proj(w_ref):
                return (jnp.dot(x_lo, w_ref[0, :half, :], preferred_element_type=F32)
                        + jnp.dot(x_hi, w_ref[0, half:, :], preferred_element_type=F32))
            hmid = (jax.nn.silu(proj(wg_ref)) * proj(wu_ref)).astype(BF16)
            return jnp.dot(hmid, wd_ref[0], preferred_element_type=F32)

        y = (gates[:, 0:1] * expert(wga_ref, wua_ref, wda_ref)
             + gates[:, 1:2] * expert(wgb_ref, wub_ref, wdb_ref))
        y_ref[...] = _pack_rows(y)

    @pl.when(i >= nused_ref[0])
    def _():
        y_ref[...] = jnp.zeros(y_ref.shape, y_ref.dtype)


def _experts(block_ea, block_eb, n_used, x_rows, w_gate, w_up, w_down):
    p, w = x_rows.shape
    tb = EXPERT_ROWS
    _, d, de = w_gate.shape
    up = lambda sel: pl.BlockSpec((1, d, de), lambda i, ea, eb, nu: (sel(ea, eb)[i], 0, 0))
    down = lambda sel: pl.BlockSpec((1, de, d), lambda i, ea, eb, nu: (sel(ea, eb)[i], 0, 0))
    first = lambda ea, eb: ea
    second = lambda ea, eb: eb
    grid_spec = pltpu.PrefetchScalarGridSpec(
        num_scalar_prefetch=3,
        grid=(p // tb,),
        in_specs=[pl.BlockSpec((tb, w), lambda i, ea, eb, nu: (i, 0)),
                  up(first), up(first), down(first), up(second), up(second), down(second)],
        out_specs=pl.BlockSpec((tb, d // 2), lambda i, ea, eb, nu: (i, 0)),
    )
    return pl.pallas_call(
        _expert_kernel,
        grid_spec=grid_spec,
        out_shape=jax.ShapeDtypeStruct((p, d // 2), jnp.uint32),
        compiler_params=_cparams("arbitrary"),
        name="experts",
    )(block_ea, block_eb, n_used, x_rows, w_gate, w_up, w_down, w_gate, w_up, w_down)


def _combine_kernel(pos_ref, posn_ref, x2_ref, g_ref, y_hbm, o_ref, ybuf, sems):
    i = pl.program_id(0)
    tm, d = x2_ref.shape
    half = d // 2
    groups = tm // SUBLANES
    slot = i % 2

    def row_copy(p, s, g, j):
        return pltpu.make_async_copy(y_hbm.at[pl.ds(p, 1), :], ybuf.at[s, g, pl.ds(j, 1), :], sems.at[s])

    def issue(p_ref, s):
        def body(g, carry):
            for j in range(SUBLANES):
                row_copy(p_ref[SUBLANES * g + j], s, g, j).start(priority=j % 2)
            return carry
        lax.fori_loop(0, groups, body, 0)

    @pl.when(i == 0)
    def _():
        issue(pos_ref, 0)

    @pl.when(i + 1 < pl.num_programs(0))
    def _():
        issue(posn_ref, 1 - slot)

    _drain(row_copy(0, slot, 0, 0), tm)

    y_lo, y_hi = _unpack_rows(ybuf[slot].reshape(tm, half))
    xl = x2_ref[:, :half] + y_lo.astype(F32)
    xh = x2_ref[:, half:] + y_hi.astype(F32)
    ms = (jnp.sum(xl * xl, axis=-1, keepdims=True) + jnp.sum(xh * xh, axis=-1, keepdims=True)) / d
    inv = lax.rsqrt(ms + RMS_EPS)
    o_ref[:, :half] = xl * inv * g_ref[:, :half]
    o_ref[:, half:] = xh * inv * g_ref[:, half:]


def _combine(pos, x2, norm_final, y_rows):
    n, d = x2.shape
    tm = MOE_TILE
    steps = n // tm
    row = lambda w: pl.BlockSpec((tm, w), lambda i: (i, 0))
    pos_spec = lambda f: pl.BlockSpec((tm,), f, memory_space=pltpu.SMEM)
    return pl.pallas_call(
        _combine_kernel,
        grid=(steps,),
        in_specs=[pos_spec(lambda i: (i,)), pos_spec(lambda i: (jnp.minimum(i + 1, steps - 1),)),
                  row(d), pl.BlockSpec(norm_final.shape, lambda i: (0, 0)),
                  pl.BlockSpec(memory_space=pl.ANY)],
        out_specs=row(d),
        out_shape=jax.ShapeDtypeStruct((n, d), F32),
        scratch_shapes=[pltpu.VMEM((2, tm // SUBLANES, SUBLANES, d // 2), jnp.uint32),
                        pltpu.SemaphoreType.DMA((2,))],
        compiler_params=_cparams("arbitrary"),
        name="combine",
    )(pos, pos, x2, norm_final, y_rows)


def _bucket_experts():
    ea, eb = [], []
    for g in range(N_GROUPS):
        for a in range(EXPERTS_PER_GROUP):
            for b in range(a + 1, EXPERTS_PER_GROUP):
                ea.append(g * EXPERTS_PER_GROUP + a)
                eb.append(g * EXPERTS_PER_GROUP + b)
    return jnp.asarray(ea, jnp.int32), jnp.asarray(eb, jnp.int32)


def _dispatch_plan(route, counts_slab, n_rows):
    tb = EXPERT_ROWS
    counts = counts_slab[0, :N_BUCKETS].astype(jnp.int32)
    padded = (counts + tb - 1) // tb * tb
    pend = jnp.cumsum(padded)
    pstart = pend - padded
    bucket = route[:, 0].astype(jnp.int32)
    rank = route[:, 1].astype(jnp.int32)
    hit = bucket[:, None] == jnp.arange(N_BUCKETS, dtype=jnp.int32)[None, :]
    pos = jnp.sum(jnp.where(hit, pstart[None, :], 0), axis=1) + rank
    nb = n_rows // tb
    block_start = jnp.arange(nb, dtype=jnp.int32) * tb
    block_bucket = jnp.minimum(jnp.sum((pend[None, :] <= block_start[:, None]).astype(jnp.int32), axis=1),
                               N_BUCKETS - 1)
    ea, eb = _bucket_experts()
    n_used = (pend[-1] // tb).astype(jnp.int32).reshape(1)
    return pos, jnp.take(ea, block_bucket), jnp.take(eb, block_bucket), n_used


def kernel(x, mem, norm_mix, w_in, sgu_ln_g, sgu_ln_b, sgu_w, sgu_b, lambda_q1, lambda_k1, lambda_q2,
           lambda_k2, diff_subln, w_out, norm_xq, norm_mem, w_q_mem, w_kv_mem, w_o_mem, norm_ffn,
           w_router_group, b_router_group, w_router_expert, b_router_expert, w_gate, w_up, w_down,
           norm_final):
    bn, sn, d = x.shape
    n = bn * sn
    assert w_in.shape[0] == 1 and sn % TOKEN_TILE == 0 and sn % ATTN_TK == 0 and n % MOE_TILE == 0
    l = 0
    x2d = x.reshape(n, d)

    sw = SGU_GROUPS * SGU_CH
    qkw = DIFF_HEADS * 2 * DIFF_QK_DIM
    w_in_l = w_in[l].astype(BF16)
    w_uvk = jnp.concatenate([w_in_l[:, :2 * sw], w_in_l[:, 2 * sw + qkw:2 * sw + 2 * qkw]], axis=1)
    w_qv_t = jnp.concatenate([w_in_l[:, 2 * sw:2 * sw + qkw], w_in_l[:, 2 * sw + 2 * qkw:]], axis=1).T
    a_out, k, qt, vt = _inproj_sgu(x2d, norm_mix[l][None], w_uvk, w_qv_t, sgu_ln_g[l], sgu_ln_b[l],
                                   sgu_w[l], sgu_b[l].T, sn)
    lamv = jnp.stack([lambda_q1[l], lambda_k1[l], lambda_q2[l], lambda_k2[l]])
    o_attn = _diff_attn(qt, k.reshape(bn, sn, k.shape[1]), vt, lamv, diff_subln[l][:, None])
    o_attn = o_attn.reshape(n, o_attn.shape[2])
    k_mem, v_mem = _mem_kv(mem, norm_mem[l][None], w_kv_mem[l].astype(BF16))

    w_r = jnp.concatenate([w_router_group[l], w_router_expert[l]], axis=1)
    w_r = jnp.pad(w_r, ((0, 0), (0, LANES - w_r.shape[1])))
    wr_hi = w_r.astype(BF16)
    wr_lo = (w_r - wr_hi.astype(F32)).astype(BF16)
    b_r = jnp.concatenate([b_router_group[l], b_router_expert[l]])
    b_r = jnp.pad(b_r, (0, LANES - b_r.shape[0]))[None]
    x2, h3p, route, counts = _postmix(x2d, a_out, o_attn, w_out[l].astype(BF16), norm_xq[l][None],
                                      w_q_mem[l].astype(BF16), k_mem, v_mem, w_o_mem[l].astype(BF16),
                                      norm_ffn[l][None], wr_hi, wr_lo, b_r, sn)

    n_rows = n + N_BUCKETS * EXPERT_ROWS
    pos, block_ea, block_eb, n_used = _dispatch_plan(route, counts, n_rows)
    x_rows = _dispatch(pos, h3p, n_rows)
    y_rows = _experts(block_ea, block_eb, n_used, x_rows, w_gate[l].astype(BF16), w_up[l].astype(BF16),
                      w_down[l].astype(BF16))
    out = _combine(pos, x2, norm_final[None], y_rows)
    return out.reshape(bn, sn, d)
```

```python
import functools
import math

import jax
import jax.numpy as jnp
from jax import lax
from jax.experimental import pallas as pl
from jax.experimental.pallas import tpu as pltpu

F32 = jnp.float32
BF16 = jnp.bfloat16

LANES = 128
SGU_GROUPS = 4
SGU_CH = 128
CHUNK = 128
DIFF_HEADS = 4
DIFF_QK_DIM = 64
DIFF_V_DIM = 128
MEM_HEADS = 4
N_GROUPS = 4
EXPERTS_PER_GROUP = 8
N_EXPERTS = N_GROUPS * EXPERTS_PER_GROUP
PAIRS_PER_GROUP = EXPERTS_PER_GROUP * (EXPERTS_PER_GROUP - 1) // 2
N_BUCKETS = N_GROUPS * PAIRS_PER_GROUP
RMS_EPS = 1e-6
LN_EPS = 1e-5
LOG2E = math.log2(math.e)
LAMBDA_INIT = 0.8 - 0.6 * math.exp(-0.3 * 0)

TOKEN_TILE = 512
ATTN_TQ = 512
ATTN_TK = 512
SLOPE_PARTS = 3
ONES_ROWS = 16
MOE_TILE = 1024
EXPERT_ROWS = 256
SUBLANES = 8
VMEM_LIMIT = 56 * 1024 * 1024


def _cparams(*sem):
    return pltpu.CompilerParams(dimension_semantics=sem, vmem_limit_bytes=VMEM_LIMIT)


def _gelu(x):
    return 0.5 * x * (1.0 + lax.erf(x * (2.0 ** -0.5)))


def _rms(x, g):
    ms = jnp.mean(x * x, axis=-1, keepdims=True)
    return x * lax.rsqrt(ms + RMS_EPS) * g


def _inproj_sgu_kernel(x_ref, g_ref, w_ref, wt_ref, lng_ref, lnb_ref, ws_ref, bs_ref,
                       a_ref, k_ref, qt_ref, vt_ref, *, tiles_per_seq):
    tm = x_ref.shape[0]
    h = _rms(x_ref[...], g_ref[...]).astype(BF16)
    sw = SGU_GROUPS * SGU_CH
    qkw = DIFF_HEADS * 2 * DIFF_QK_DIM

    def proj(c0, c1):
        return jnp.dot(h, w_ref[:, c0:c1], preferred_element_type=F32)

    def proj_t(r0, r1):
        return lax.dot_general(wt_ref[r0:r1, :], h, (((1,), (1,)), ((), ())), preferred_element_type=F32)

    u = _gelu(proj(0, sw))
    v = _gelu(proj(sw, 2 * sw))
    kf = proj(2 * sw, 2 * sw + qkw)
    pos = (pl.program_id(0) % tiles_per_seq) * tm + lax.broadcasted_iota(jnp.int32, (tm, LANES), 0)
    lane = lax.broadcasted_iota(jnp.int32, (tm, LANES), 1)
    pos_hi = ((pos >> 8) << 8).astype(F32)
    pos_lo = (pos & 255).astype(F32)
    aug = jnp.where(lane < SLOPE_PARTS, pos_hi, jnp.where(lane < 2 * SLOPE_PARTS, pos_lo, 0.0))
    pieces = []
    for hh in range(DIFF_HEADS):
        pieces += [kf[:, hh * 2 * DIFF_QK_DIM:(hh + 1) * 2 * DIFF_QK_DIM], aug]
    k_ref[...] = jnp.concatenate(pieces, axis=1).astype(BF16)
    qt_ref[0] = (proj_t(0, qkw) * (DIFF_QK_DIM ** -0.5 * LOG2E)).astype(BF16)
    vt = proj_t(qkw, qkw + DIFF_HEADS * DIFF_V_DIM)
    ones = jnp.ones((ONES_ROWS, tm), F32)
    pieces = []
    for hh in range(DIFF_HEADS):
        pieces += [vt[hh * DIFF_V_DIM:(hh + 1) * DIFF_V_DIM, :], ones]
    vt_ref[0] = jnp.concatenate(pieces, axis=0).astype(BF16)

    row = lax.broadcasted_iota(jnp.int32, (CHUNK, CHUNK), 0)
    col = lax.broadcasted_iota(jnp.int32, (CHUNK, CHUNK), 1)
    nchunk = tm // CHUNK
    for g in range(SGU_GROUPS):
        cs = slice(g * SGU_CH, (g + 1) * SGU_CH)
        vg = v[:, cs]
        mu = jnp.mean(vg, axis=-1, keepdims=True)
        d = vg - mu
        var = jnp.mean(d * d, axis=-1, keepdims=True)
        vn = (d * lax.rsqrt(var + LN_EPS) * lng_ref[g:g + 1, :] + lnb_ref[g:g + 1, :]).astype(BF16)
        rhs = jnp.concatenate([vn[c * CHUNK:(c + 1) * CHUNK, :] for c in range(nchunk)], axis=1)
        wt = jnp.where(row >= col, ws_ref[g], 0.0).astype(BF16)
        sp = jnp.dot(wt, rhs, preferred_element_type=F32) + bs_ref[:, g:g + 1]
        for c in range(nchunk):
            rs = slice(c * CHUNK, (c + 1) * CHUNK)
            a_ref[rs, cs] = (u[rs, cs] * sp[:, c * SGU_CH:(c + 1) * SGU_CH]).astype(BF16)


def _inproj_sgu(x2d, norm_mix, w_uvk, w_qv_t, ln_g, ln_b, w_s, b_s_t, seq):
    n, d = x2d.shape
    tm = TOKEN_TILE
    per_b = seq // tm
    wide = DIFF_HEADS * DIFF_V_DIM
    k_cols = DIFF_HEADS * (2 * DIFF_QK_DIM + LANES)
    row_spec = pl.BlockSpec((tm, wide), lambda i: (i, 0))
    vt_rows = DIFF_HEADS * (DIFF_V_DIM + ONES_ROWS)
    t_spec = lambda rows: pl.BlockSpec((1, rows, tm), lambda i: (i // per_b, 0, i % per_b))
    full = lambda a: pl.BlockSpec(a.shape, lambda i: (0,) * a.ndim)
    row_sds = jax.ShapeDtypeStruct((n, wide), BF16)
    t_sds = lambda rows: jax.ShapeDtypeStruct((n // seq, rows, seq), BF16)
    return pl.pallas_call(
        functools.partial(_inproj_sgu_kernel, tiles_per_seq=per_b),
        grid=(n // tm,),
        in_specs=[pl.BlockSpec((tm, d), lambda i: (i, 0)), full(norm_mix), full(w_uvk), full(w_qv_t),
                  full(ln_g), full(ln_b), full(w_s), full(b_s_t)],
        out_specs=[row_spec, pl.BlockSpec((tm, k_cols), lambda i: (i, 0)), t_spec(wide), t_spec(vt_rows)],
        out_shape=[row_sds, jax.ShapeDtypeStruct((n, k_cols), BF16), t_sds(wide), t_sds(vt_rows)],
        compiler_params=_cparams("parallel"),
        name="inproj_sgu",
    )(x2d, norm_mix, w_uvk, w_qv_t, ln_g, ln_b, w_s, b_s_t)


def _diff_attn_kernel(qt_ref, k_ref, vt_ref, lamv_ref, sub_ref, o_ref, mask_scr, acc_scr, s0_scr, s1_scr):
    dqk = DIFF_QK_DIM
    tq = qt_ref.shape[2]
    tk = mask_scr.shape[1]
    h = pl.program_id(1)
    iq = pl.program_id(2)
    slope2 = LOG2E * jnp.exp2(-(8.0 / DIFF_HEADS) * jnp.full((1, 1), h + 1, jnp.int32).astype(F32))

    qt = qt_ref[0]
    frow = lax.broadcasted_iota(jnp.int32, qt.shape, 0)
    zero = jnp.zeros_like(qt)
    qw = jnp.concatenate([jnp.where(frow < dqk, qt, zero), jnp.where(frow >= dqk, qt, zero)], axis=1)
    arow = lax.broadcasted_iota(jnp.int32, (LANES, 2 * tq), 0)
    coef = jnp.zeros((LANES, 2 * tq), F32)
    rem = slope2
    for i in range(SLOPE_PARTS):
        piece = rem.astype(BF16).astype(F32)
        rem = rem - piece
        coef = jnp.where(arow == i, piece, jnp.where(arow == SLOPE_PARTS + i, piece, coef))
    qw = jnp.concatenate([qw, coef.astype(BF16)], axis=0)
    nvar = tk // tq

    @pl.when(iq == 0)
    def _():
        krow = lax.broadcasted_iota(jnp.int32, (tk, 2 * tq), 0)
        qcol = lax.broadcasted_iota(jnp.int32, (tk, 2 * tq), 1) % tq
        mask_scr[0] = jnp.zeros((tk, 2 * tq), F32)
        for v in range(nvar):
            mask_scr[1 + v] = jnp.where(krow - v * tq <= qcol, 0.0, -jnp.inf)

    acc_scr[...] = jnp.zeros(acc_scr.shape, F32)
    n_full = (iq * tq) // tk
    last_mask = 1 + iq % nvar

    def scores(j, buf, mask_sel=None):
        off = pl.multiple_of(j * tk, tk)
        s = jnp.dot(k_ref[0, pl.ds(off, tk), :], qw, preferred_element_type=F32)
        if mask_sel is not None:
            s = s + mask_scr[mask_sel]
        buf[...] = s
        return jnp.max(s, axis=0, keepdims=True)

    def accumulate(j, buf, bm, m_prev):
        off = pl.multiple_of(j * tk, tk)
        m_new = jnp.maximum(m_prev, bm)
        p = jnp.exp2(buf[...] - m_new)
        alpha = jnp.exp2(m_prev - m_new)
        pv = jnp.dot(vt_ref[0, :, pl.ds(off, tk)], p.astype(BF16), preferred_element_type=F32)
        acc_scr[...] = alpha * acc_scr[...] + pv
        return m_new

    def body(t, c):
        m, bm0 = c
        bm1 = scores(2 * t + 1, s1_scr)
        m = accumulate(2 * t, s0_scr, bm0, m)
        bm0 = scores(2 * t + 2, s0_scr)
        m = accumulate(2 * t + 1, s1_scr, bm1, m)
        return m, bm0

    m0 = jnp.full((1, 2 * tq), -jnp.inf, F32)
    bm0 = scores(0, s0_scr, jnp.where(n_full == 0, last_mask, 0))
    n_loop = jnp.maximum(n_full - 1, 0) // 2
    m, bm0 = lax.fori_loop(0, n_loop, body, (m0, bm0))
    rest = 2 * n_loop
    left = n_full - rest

    @pl.when(left == 0)
    def _():
        accumulate(rest, s0_scr, bm0, m)

    @pl.when(left == 1)
    def _():
        bm1 = scores(rest + 1, s1_scr, last_mask)
        m1 = accumulate(rest, s0_scr, bm0, m)
        accumulate(rest + 1, s1_scr, bm1, m1)

    @pl.when(left == 2)
    def _():
        bm1 = scores(rest + 1, s1_scr)
        m1 = accumulate(rest, s0_scr, bm0, m)
        bm2 = scores(rest + 2, s0_scr, last_mask)
        m2 = accumulate(rest + 1, s1_scr, bm1, m1)
        accumulate(rest + 2, s0_scr, bm2, m2)

    lam = (jnp.exp(jnp.sum(lamv_ref[0:1, :] * lamv_ref[1:2, :], axis=-1, keepdims=True))
           - jnp.exp(jnp.sum(lamv_ref[2:3, :] * lamv_ref[3:4, :], axis=-1, keepdims=True))
           + LAMBDA_INIT)
    dv = DIFF_V_DIM
    on = acc_scr[:dv, :] * (1.0 / acc_scr[dv:dv + 1, :])
    o = on[:, :tq] - lam * on[:, tq:]
    ms = jnp.mean(o * o, axis=0, keepdims=True)
    o = o * lax.rsqrt(ms + RMS_EPS) * sub_ref[...] * (1.0 - LAMBDA_INIT)
    o_ref[0] = o.T.astype(o_ref.dtype)


def _diff_attn(qt, k, vt, lamv, subln_col):
    b, s, _ = k.shape
    tq, tk = ATTN_TQ, ATTN_TK
    dv = DIFF_V_DIM
    assert tk % tq == 0 and s % tk == 0
    return pl.pallas_call(
        _diff_attn_kernel,
        grid=(b, DIFF_HEADS, s // tq),
        in_specs=[pl.BlockSpec((1, 2 * DIFF_QK_DIM, tq), lambda bi, hi, qi: (bi, hi, qi)),
                  pl.BlockSpec((1, s, 2 * DIFF_QK_DIM + LANES), lambda bi, hi, qi: (bi, 0, hi)),
                  pl.BlockSpec((1, dv + ONES_ROWS, s), lambda bi, hi, qi: (bi, hi, 0)),
                  pl.BlockSpec(lamv.shape, lambda bi, hi, qi: (0, 0)),
                  pl.BlockSpec(subln_col.shape, lambda bi, hi, qi: (0, 0))],
        out_specs=pl.BlockSpec((1, tq, dv), lambda bi, hi, qi: (bi, qi, hi)),
        out_shape=jax.ShapeDtypeStruct((b, s, DIFF_HEADS * dv), BF16),
        scratch_shapes=[pltpu.VMEM((1 + tk // tq, tk, 2 * tq), F32),
                        pltpu.VMEM((dv + ONES_ROWS, 2 * tq), F32),
                        pltpu.VMEM((tk, 2 * tq), F32), pltpu.VMEM((tk, 2 * tq), F32)],
        compiler_params=_cparams("arbitrary", "arbitrary", "arbitrary"),
        name="diff_attn",
    )(qt, k, vt, lamv, subln_col)


def _mem_kv_kernel(mem_ref, g_ref, w_ref, k_ref, v_ref):
    d = mem_ref.shape[-1]
    mn = _rms(mem_ref[0], g_ref[...]).astype(BF16)
    k_ref[0] = jnp.dot(mn, w_ref[:, :d], preferred_element_type=F32).astype(BF16)
    v_ref[0] = jnp.dot(mn, w_ref[:, d:], preferred_element_type=F32).astype(BF16)


def _mem_kv(mem, norm_mem, w_kv):
    b, m, d = mem.shape
    spec = pl.BlockSpec((1, m, d), lambda i: (i, 0, 0))
    sds = jax.ShapeDtypeStruct((b, m, d), BF16)
    return pl.pallas_call(
        _mem_kv_kernel,
        grid=(b,),
        in_specs=[spec, pl.BlockSpec(norm_mem.shape, lambda i: (0, 0)),
                  pl.BlockSpec(w_kv.shape, lambda i: (0, 0))],
        out_specs=[spec, spec],
        out_shape=[sds, sds],
        compiler_params=_cparams("parallel"),
        name="mem_kv",
    )(mem, norm_mem, w_kv)


def _pack_rows(x):
    w = x.shape[1] // 2
    bits = pltpu.bitcast(x.astype(BF16).astype(F32), jnp.uint32)
    return (bits[:, :w] >> 16) | bits[:, w:]


def _unpack_rows(u):
    lo = pltpu.bitcast(u << 16, F32).astype(BF16)
    hi = pltpu.bitcast(u & jnp.uint32(0xFFFF0000), F32).astype(BF16)
    return lo, hi


def _postmix_kernel(x_ref, a_ref, o_ref, wout_ref, gq_ref, wq_ref, km_ref, vm_ref, wo_ref,
                    gf_ref, wrh_ref, wrl_ref, br_ref, x2_ref, h3_ref, route_ref, counts_ref,
                    ltri_scr, run_scr):
    tm, d = x_ref.shape

    @pl.when(pl.program_id(0) == 0)
    def _():
        r = lax.broadcasted_iota(jnp.int32, (tm, tm), 0)
        c = lax.broadcasted_iota(jnp.int32, (tm, tm), 1)
        ltri_scr[...] = jnp.where(c < r, 1.0, 0.0).astype(BF16)
        run_scr[...] = jnp.zeros(run_scr.shape, F32)

    half = a_ref.shape[1]
    hd = d // MEM_HEADS
    mixed = (jnp.dot(a_ref[...], wout_ref[:half, :], preferred_element_type=F32)
             + jnp.dot(o_ref[...], wout_ref[half:, :], preferred_element_type=F32))
    x1 = x_ref[...] + mixed

    hq = _rms(x1, gq_ref[...]).astype(BF16)
    q = (jnp.dot(hq, wq_ref[...], preferred_element_type=F32) * (hd ** -0.5)).astype(BF16)
    heads = []
    for hh in range(MEM_HEADS):
        cs = slice(hh * hd, (hh + 1) * hd)
        s = lax.dot_general(q[:, cs], km_ref[0, :, cs], (((1,), (1,)), ((), ())),
                            preferred_element_type=F32)
        e = jnp.exp(s - jnp.max(s, axis=-1, keepdims=True))
        p = e / jnp.sum(e, axis=-1, keepdims=True)
        heads.append(jnp.dot(p.astype(BF16), vm_ref[0, :, cs], preferred_element_type=F32).astype(BF16))
    x2 = x1 + jnp.dot(jnp.concatenate(heads, axis=1), wo_ref[...], preferred_element_type=F32)
    x2_ref[...] = x2

    h3 = _rms(x2, gf_ref[...])
    h3_hi = h3.astype(BF16)
    h3_lo = (h3 - h3_hi.astype(F32)).astype(BF16)
    lg = (jnp.dot(h3_hi, wrh_ref[...], preferred_element_type=F32)
          + jnp.dot(h3_lo, wrh_ref[...], preferred_element_type=F32)
          + jnp.dot(h3_hi, wrl_ref[...], preferred_element_type=F32)) + br_ref[...]

    lane = lax.broadcasted_iota(jnp.int32, lg.shape, 1)
    lanef = lane.astype(F32)
    big = float(LANES)
    ninf = -jnp.inf
    gl = jnp.where(lane < N_GROUPS, lg, ninf)
    gmax = jnp.max(gl, axis=-1, keepdims=True)
    gidx = jnp.min(jnp.where(gl == gmax, lanef, big), axis=-1, keepdims=True)
    gate = 1.0 / jnp.sum(jnp.exp(gl - gmax), axis=-1, keepdims=True)
    lo = N_GROUPS + EXPERTS_PER_GROUP * gidx
    el = jnp.where((lanef >= lo) & (lanef < lo + EXPERTS_PER_GROUP), lg, ninf)
    e1 = jnp.max(el, axis=-1, keepdims=True)
    i1 = jnp.min(jnp.where(el == e1, lanef, big), axis=-1, keepdims=True)
    el2 = jnp.where(lanef == i1, ninf, el)
    e2 = jnp.max(el2, axis=-1, keepdims=True)
    i2 = jnp.min(jnp.where(el2 == e2, lanef, big), axis=-1, keepdims=True)
    tt = jnp.exp(e2 - e1)
    w1 = gate / (1.0 + tt)
    w2 = gate * tt / (1.0 + tt)

    e_lo = jnp.minimum(i1, i2) - lo
    e_hi = jnp.maximum(i1, i2) - lo
    pair = e_lo * (2 * EXPERTS_PER_GROUP - 1 - e_lo) * 0.5 + (e_hi - e_lo - 1.0)
    bucket = gidx * PAIRS_PER_GROUP + pair
    first_is_lo = i1 < i2
    w_lo = jnp.where(first_is_lo, w1, w2)
    w_hi = jnp.where(first_is_lo, w2, w1)

    hit = lanef == bucket
    cnt = jnp.where(hit, 1.0, 0.0)
    run = run_scr[0:1, :]
    before = jnp.dot(ltri_scr[...], cnt.astype(BF16), preferred_element_type=F32) + run
    rank = jnp.sum(jnp.where(hit, before, 0.0), axis=-1, keepdims=True)
    run_scr[...] = jnp.broadcast_to(run + jnp.sum(cnt, axis=0, keepdims=True), run_scr.shape)
    counts_ref[...] = run_scr[...]

    slab = jnp.zeros(lg.shape, F32)
    for li, val in enumerate((bucket, rank)):
        slab = jnp.where(lane == li, val, slab)
    route_ref[...] = slab

    gates = jnp.where(lane == 0, w_lo, jnp.where(lane == 1, w_hi, 0.0))
    h3_ref[...] = jnp.concatenate([_pack_rows(h3), pltpu.bitcast(gates, jnp.uint32)], axis=1)


def _postmix(x2d, a_out, o_attn, w_out, norm_xq, w_q, k_mem, v_mem, w_o, norm_ffn,
             wr_hi, wr_lo, b_r, seq):
    n, d = x2d.shape
    tm = TOKEN_TILE
    per_b = seq // tm
    half = a_out.shape[1]
    row = lambda w: pl.BlockSpec((tm, w), lambda i: (i, 0))
    full = lambda a: pl.BlockSpec(a.shape, lambda i: (0,) * a.ndim)
    memspec = pl.BlockSpec((1,) + k_mem.shape[1:], lambda i: (i // per_b, 0, 0))
    return pl.pallas_call(
        _postmix_kernel,
        grid=(n // tm,),
        in_specs=[row(d), row(half), row(half), full(w_out), full(norm_xq), full(w_q),
                  memspec, memspec, full(w_o), full(norm_ffn), full(wr_hi), full(wr_lo), full(b_r)],
        out_specs=[row(d), row(d // 2 + LANES), row(LANES), pl.BlockSpec((8, LANES), lambda i: (0, 0))],
        out_shape=[jax.ShapeDtypeStruct((n, d), F32), jax.ShapeDtypeStruct((n, d // 2 + LANES), jnp.uint32),
                   jax.ShapeDtypeStruct((n, LANES), F32), jax.ShapeDtypeStruct((8, LANES), F32)],
        scratch_shapes=[pltpu.VMEM((tm, tm), BF16), pltpu.VMEM((8, LANES), F32)],
        compiler_params=_cparams("arbitrary"),
        name="postmix",
    )(x2d, a_out, o_attn, w_out, norm_xq, w_q, k_mem, v_mem, w_o, norm_ffn, wr_hi, wr_lo, b_r)


def _drain(copy, count):
    def body(t, carry):
        copy.wait()
        return carry
    lax.fori_loop(0, count, body, 0, unroll=8)


def _dispatch_kernel(pos_ref, h_ref, xin_ref, xout_ref, sem):
    del xin_ref
    groups = h_ref.shape[0]

    def row_copy(g, j, p):
        return pltpu.make_async_copy(h_ref.at[g, pl.ds(j, 1), :], xout_ref.at[pl.ds(p, 1), :], sem)

    def issue(g, carry):
        for j in range(SUBLANES):
            row_copy(g, j, pos_ref[SUBLANES * g + j]).start(priority=j % 2)
        return carry

    lax.fori_loop(0, groups, issue, 0)
    _drain(row_copy(0, 0, 0), groups * SUBLANES)


def _dispatch(pos, h3p, n_rows):
    n, w = h3p.shape
    tm = MOE_TILE
    x0 = jnp.zeros((n_rows, w), jnp.uint32)
    return pl.pallas_call(
        _dispatch_kernel,
        grid=(n // tm,),
        in_specs=[pl.BlockSpec((tm,), lambda i: (i,), memory_space=pltpu.SMEM),
                  pl.BlockSpec((tm // SUBLANES, SUBLANES, w), lambda i: (i, 0, 0)),
                  pl.BlockSpec(memory_space=pl.ANY)],
        out_specs=pl.BlockSpec(memory_space=pl.ANY),
        out_shape=jax.ShapeDtypeStruct((n_rows, w), jnp.uint32),
        scratch_shapes=[pltpu.SemaphoreType.DMA(())],
        input_output_aliases={2: 0},
        compiler_params=_cparams("arbitrary"),
        name="dispatch",
    )(pos, h3p.reshape(n // SUBLANES, SUBLANES, w), x0)


def _expert_kernel(ea_ref, eb_ref, nused_ref, x_ref, wga_ref, wua_ref, wda_ref, wgb_ref, wub_ref, wdb_ref,
                   y_ref):
    del ea_ref, eb_ref
    i = pl.program_id(0)
    half = y_ref.shape[1]

    @pl.when(i < nused_ref[0])
    def _():
        x_lo, x_hi = _unpack_rows(x_ref[:, :half])
        gates = pltpu.bitcast(x_ref[:, half:], F32)

        def expert(wg_ref, wu_ref, wd_ref):
            def proj(w_ref):
                return (jnp.dot(x_lo, w_ref[0, :half, :], preferred_element_type=F32)
                        + jnp.dot(x_hi, w_ref[0, half:, :], preferred_element_type=F32))
            hmid = (jax.nn.silu(proj(wg_ref)) * proj(wu_ref)).astype(BF16)
            return jnp.dot(hmid, wd_ref[0], preferred_element_type=F32)

        y = (gates[:, 0:1] * expert(wga_ref, wua_ref, wda_ref)
             + gates[:, 1:2] * expert(wgb_ref, wub_ref, wdb_ref))
        y_ref[...] = _pack_rows(y)

    @pl.when(i >= nused_ref[0])
    def _():
        y_ref[...] = jnp.zeros(y_ref.shape, y_ref.dtype)


def _experts(block_ea, block_eb, n_used, x_rows, w_gate, w_up, w_down):
    p, w = x_rows.shape
    tb = EXPERT_ROWS
    _, d, de = w_gate.shape
    up = lambda sel: pl.BlockSpec((1, d, de), lambda i, ea, eb, nu: (sel(ea, eb)[i], 0, 0))
    down = lambda sel: pl.BlockSpec((1, de, d), lambda i, ea, eb, nu: (sel(ea, eb)[i], 0, 0))
    first = lambda ea, eb: ea
    second = lambda ea, eb: eb
    grid_spec = pltpu.PrefetchScalarGridSpec(
        num_scalar_prefetch=3,
        grid=(p // tb,),
        in_specs=[pl.BlockSpec((tb, w), lambda i, ea, eb, nu: (i, 0)),
                  up(first), up(first), down(first), up(second), up(second), down(second)],
        out_specs=pl.BlockSpec((tb, d // 2), lambda i, ea, eb, nu: (i, 0)),
    )
    return pl.pallas_call(
        _expert_kernel,
        grid_spec=grid_spec,
        out_shape=jax.ShapeDtypeStruct((p, d // 2), jnp.uint32),
        compiler_params=_cparams("arbitrary"),
        name="experts",
    )(block_ea, block_eb, n_used, x_rows, w_gate, w_up, w_down, w_gate, w_up, w_down)


def _combine_kernel(pos_ref, posn_ref, x2_ref, g_ref, y_hbm, o_ref, ybuf, sems):
    i = pl.program_id(0)
    tm, d = x2_ref.shape
    half = d // 2
    groups = tm // SUBLANES
    slot = i % 2

    def row_copy(p, s, g, j):
        return pltpu.make_async_copy(y_hbm.at[pl.ds(p, 1), :], ybuf.at[s, g, pl.ds(j, 1), :], sems.at[s])

    def issue(p_ref, s):
        def body(g, carry):
            for j in range(SUBLANES):
                row_copy(p_ref[SUBLANES * g + j], s, g, j).start(priority=j % 2)
            return carry
        lax.fori_loop(0, groups, body, 0)

    @pl.when(i == 0)
    def _():
        issue(pos_ref, 0)

    @pl.when(i + 1 < pl.num_programs(0))
    def _():
        issue(posn_ref, 1 - slot)

    _drain(row_copy(0, slot, 0, 0), tm)

    y_lo, y_hi = _unpack_rows(ybuf[slot].reshape(tm, half))
    xl = x2_ref[:, :half] + y_lo.astype(F32)
    xh = x2_ref[:, half:] + y_hi.astype(F32)
    ms = (jnp.sum(xl * xl, axis=-1, keepdims=True) + jnp.sum(xh * xh, axis=-1, keepdims=True)) / d
    inv = lax.rsqrt(ms + RMS_EPS)
    o_ref[:, :half] = xl * inv * g_ref[:, :half]
    o_ref[:, half:] = xh * inv * g_ref[:, half:]


def _combine(pos, x2, norm_final, y_rows):
    n, d = x2.shape
    tm = MOE_TILE
    steps = n // tm
    row = lambda w: pl.BlockSpec((tm, w), lambda i: (i, 0))
    pos_spec = lambda f: pl.BlockSpec((tm,), f, memory_space=pltpu.SMEM)
    return pl.pallas_call(
        _combine_kernel,
        grid=(steps,),
        in_specs=[pos_spec(lambda i: (i,)), pos_spec(lambda i: (jnp.minimum(i + 1, steps - 1),)),
                  row(d), pl.BlockSpec(norm_final.shape, lambda i: (0, 0)),
                  pl.BlockSpec(memory_space=pl.ANY)],
        out_specs=row(d),
        out_shape=jax.ShapeDtypeStruct((n, d), F32),
        scratch_shapes=[pltpu.VMEM((2, tm // SUBLANES, SUBLANES, d // 2), jnp.uint32),
                        pltpu.SemaphoreType.DMA((2,))],
        compiler_params=_cparams("arbitrary"),
        name="combine",
    )(pos, pos, x2, norm_final, y_rows)


def _bucket_experts():
    ea, eb = [], []
    for g in range(N_GROUPS):
        for a in range(EXPERTS_PER_GROUP):
            for b in range(a + 1, EXPERTS_PER_GROUP):
                ea.append(g * EXPERTS_PER_GROUP + a)
                eb.append(g * EXPERTS_PER_GROUP + b)
    return jnp.asarray(ea, jnp.int32), jnp.asarray(eb, jnp.int32)


def _dispatch_plan(route, counts_slab, n_rows):
    tb = EXPERT_ROWS
    counts = counts_slab[0, :N_BUCKETS].astype(jnp.int32)
    padded = (counts + tb - 1) // tb * tb
    pend = jnp.cumsum(padded)
    pstart = pend - padded
    bucket = route[:, 0].astype(jnp.int32)
    rank = route[:, 1].astype(jnp.int32)
    hit = bucket[:, None] == jnp.arange(N_BUCKETS, dtype=jnp.int32)[None, :]
    pos = jnp.sum(jnp.where(hit, pstart[None, :], 0), axis=1) + rank
    nb = n_rows // tb
    block_start = jnp.arange(nb, dtype=jnp.int32) * tb
    block_bucket = jnp.minimum(jnp.sum((pend[None, :] <= block_start[:, None]).astype(jnp.int32), axis=1),
                               N_BUCKETS - 1)
    ea, eb = _bucket_experts()
    n_used = (pend[-1] // tb).astype(jnp.int32).reshape(1)
    return pos, jnp.take(ea, block_bucket), jnp.take(eb, block_bucket), n_used


def kernel(x, mem, norm_mix, w_in, sgu_ln_g, sgu_ln_b, sgu_w, sgu_b, lambda_q1, lambda_k1, lambda_q2,
           lambda_k2, diff_subln, w_out, norm_xq, norm_mem, w_q_mem, w_kv_mem, w_o_mem, norm_ffn,
           w_router_group, b_router_group, w_router_expert, b_router_expert, w_gate, w_up, w_down,
           norm_final):
    bn, sn, d = x.shape
    n = bn * sn
    assert w_in.shape[0] == 1 and sn % TOKEN_TILE == 0 and sn % ATTN_TK == 0 and n % MOE_TILE == 0
    l = 0
    x2d = x.reshape(n, d)

    sw = SGU_GROUPS * SGU_CH
    qkw = DIFF_HEADS * 2 * DIFF_QK_DIM
    w_in_l = w_in[l].astype(BF16)
    w_uvk = jnp.concatenate([w_in_l[:, :2 * sw], w_in_l[:, 2 * sw + qkw:2 * sw + 2 * qkw]], axis=1)
    w_qv_t = jnp.concatenate([w_in_l[:, 2 * sw:2 * sw + qkw], w_in_l[:, 2 * sw + 2 * qkw:]], axis=1).T
    a_out, k, qt, vt = _inproj_sgu(x2d, norm_mix[l][None], w_uvk, w_qv_t, sgu_ln_g[l], sgu_ln_b[l],
                                   sgu_w[l], sgu_b[l].T, sn)
    lamv = jnp.stack([lambda_q1[l], lambda_k1[l], lambda_q2[l], lambda_k2[l]])
    o_attn = _diff_attn(qt, k.reshape(bn, sn, k.shape[1]), vt, lamv, diff_subln[l][:, None])
    o_attn = o_attn.reshape(n, o_attn.shape[2])
    k_mem, v_mem = _mem_kv(mem, norm_mem[l][None], w_kv_mem[l].astype(BF16))

    w_r = jnp.concatenate([w_router_group[l], w_router_expert[l]], axis=1)
    w_r = jnp.pad(w_r, ((0, 0), (0, LANES - w_r.shape[1])))
    wr_hi = w_r.astype(BF16)
    wr_lo = (w_r - wr_hi.astype(F32)).astype(BF16)
    b_r = jnp.concatenate([b_router_group[l], b_router_expert[l]])
    b_r = jnp.pad(b_r, (0, LANES - b_r.shape[0]))[None]
    x2, h3p, route, counts = _postmix(x2d, a_out, o_attn, w_out[l].astype(BF16), norm_xq[l][None],
                                      w_q_mem[l].astype(BF16), k_mem, v_mem, w_o_mem[l].astype(BF16),
                                      norm_ffn[l][None], wr_hi, wr_lo, b_r, sn)

    n_rows = n + N_BUCKETS * EXPERT_ROWS
    pos, block_ea, block_eb, n_used = _dispatch_plan(route, counts, n_rows)
    x_rows = _dispatch(pos, h3p, n_rows)
    y_rows = _experts(block_ea, block_eb, n_used, x_rows, w_gate[l].astype(BF16), w_up[l].astype(BF16),
                      w_down[l].astype(BF16))
    out = _combine(pos, x2, norm_final[None], y_rows)
    return out.reshape(bn, sn, d)
```
